```python
import math
import jax, jax.numpy as jnp
from jax import lax
import numpy as np

D_MODEL = 2048
BATCH = 2
SEQ = 4096
DEPTH = 4
DEC_BATCH = 8
DEC_SEQ = 8
PAST_LEN = 16384
PAGE_SIZE = 128

N_HEADS = 16
HEAD_DIM = 64
N_KV = 4
HPG = N_HEADS // N_KV
CMP_BLOCK = 32
SEL_BLOCK = 64
TOP_K = 16
WINDOW = 512
Q_BLOCK = 128
RET_HEADS = 4
RET_DK = 256
RET_DV = 512
RET_CHUNK = 128
ROPE_BASE = 10000.0
D_FF = 4 * D_MODEL
N_BUCKETS = 32
REL_MAX_DIST = 2048

NORM_EPS = 1e-6
FORCE_BONUS = 1e6
NEG = -1e30

SPLIT_SIZES = (N_HEADS * HEAD_DIM,) + (N_KV * HEAD_DIM,) * 6 + (3 * N_HEADS, RET_HEADS * RET_DK, RET_HEADS * RET_DK, RET_HEADS * RET_DV, RET_HEADS * RET_DV, D_MODEL, D_MODEL)
D_IN = sum(SPLIT_SIZES)

kernel_name = "nsa_retention_parallel_hybrid_step"


def rmsnorm(x, g):
    xf = x.astype(jnp.float32)
    y = xf * lax.rsqrt(jnp.mean(xf * xf, axis=-1, keepdims=True) + NORM_EPS)
    return (y * g.astype(jnp.float32)).astype(x.dtype)


def split_in(z):
    cuts, acc = [], 0
    for s in SPLIT_SIZES[:-1]:
        acc += s
        cuts.append(acc)
    return jnp.split(z, cuts, axis=-1)


def t5_bucket(dist):
    n = jnp.maximum(dist, 0)
    exact = N_BUCKETS // 2
    nf = jnp.maximum(n, exact).astype(jnp.float32)
    large = exact + (jnp.log(nf / exact) / math.log(REL_MAX_DIST / exact) * (N_BUCKETS - exact)).astype(jnp.int32)
    large = jnp.minimum(large, N_BUCKETS - 1)
    return jnp.where(n < exact, n, large)


def masked_softmax(s, mask):
    p = jax.nn.softmax(jnp.where(mask, s, NEG), axis=-1)
    return jnp.where(mask, p, 0.0)


def compress(raw, pos_emb, w_phi):
    B, Lc = raw.shape[:2]
    n = Lc // CMP_BLOCK
    blocks = raw.reshape(B, n, CMP_BLOCK, N_KV, HEAD_DIM) + pos_emb[None, None, :, None, :]
    return jnp.einsum('bncgd,cde->bnge', blocks, w_phi)


def nsa_attend(q, gates, t_pos, kc, vc, ks, vs, kw, vw, kw0, rel_bias):
    f32 = jnp.float32
    B, nq = q.shape[:2]
    qg = q.reshape(B, nq, N_KV, HPG, HEAD_DIM)
    table = rel_bias.reshape(N_BUCKETS, N_KV, HPG)

    n_cmp = kc.shape[1]
    c_end = jnp.arange(n_cmp, dtype=jnp.int32) * CMP_BLOCK + (CMP_BLOCK - 1)
    dist_c = t_pos[:, None] - c_end[None, :]
    bias_c = jnp.transpose(table[t5_bucket(dist_c)], (0, 2, 3, 1))
    s_c = jnp.einsum('bqghd,bkgd->bqghk', qg, kc).astype(f32) + bias_c
    p_c = masked_softmax(s_c, (dist_c >= 0)[None, :, None, None, :])
    o_c = jnp.einsum('bqghk,bkgd->bqghd', p_c.astype(vc.dtype), vc)

    n_sel = ks.shape[1] // SEL_BLOCK
    ratio = SEL_BLOCK // CMP_BLOCK
    imp = jnp.pad(p_c.sum(3), ((0, 0), (0, 0), (0, 0), (0, n_sel * ratio - n_cmp)))
    imp = imp.reshape(B, nq, N_KV, n_sel, ratio).sum(-1)
    blk = jnp.arange(n_sel, dtype=jnp.int32)[None, :]
    cur = (t_pos // SEL_BLOCK)[:, None]
    forced = (blk == 0) | (blk == cur) | (blk == cur - 1)
    valid = blk * SEL_BLOCK <= t_pos[:, None]
    score = imp + jnp.where(forced, FORCE_BONUS, 0.0)[None, :, None, :]
    score = jnp.where(valid[None, :, None, :], score, NEG)
    k_eff = min(TOP_K, n_sel)
    _, idx = lax.top_k(score, k_eff)
    pos = (idx[..., None] * SEL_BLOCK + jnp.arange(SEL_BLOCK, dtype=jnp.int32)).reshape(B, nq, N_KV, k_eff * SEL_BLOCK)
    bi = jnp.arange(B)[:, None, None, None]
    gi = jnp.arange(N_KV)[None, None, :, None]
    k_g = ks[bi, pos, gi]
    v_g = vs[bi, pos, gi]
    dist_s = t_pos[None, :, None, None] - pos
    bias_s = jnp.moveaxis(table[t5_bucket(dist_s), gi], -1, 3)
    s_s = jnp.einsum('bqghd,bqgnd->bqghn', qg, k_g).astype(f32) + bias_s
    p_s = masked_softmax(s_s, (dist_s >= 0)[:, :, :, None, :])
    o_s = jnp.einsum('bqghn,bqgnd->bqghd', p_s.astype(v_g.dtype), v_g)

    kpos = kw0 + jnp.arange(kw.shape[1], dtype=jnp.int32)
    dist_w = t_pos[:, None] - kpos[None, :]
    mask_w = (dist_w >= 0) & (dist_w < WINDOW) & (kpos[None, :] >= 0)
    bias_w = jnp.transpose(table[t5_bucket(dist_w)], (0, 2, 3, 1))
    s_w = jnp.einsum('bqghd,bkgd->bqghk', qg, kw).astype(f32) + bias_w
    p_w = masked_softmax(s_w, mask_w[None, :, None, None, :])
    o_w = jnp.einsum('bqghk,bkgd->bqghd', p_w.astype(vw.dtype), vw)

    g = gates.reshape(B, nq, N_KV, HPG, 3)
    o = g[..., 0:1] * o_c + g[..., 1:2] * o_s + g[..., 2:3] * o_w
    return o.reshape(B, nq, N_HEADS, HEAD_DIM)


def rotate(x, pos):
    half = x.shape[-1] // 2
    inv = ROPE_BASE ** (-jnp.arange(half, dtype=jnp.float32) / half)
    ang = pos.astype(jnp.float32)[:, None] * inv[None, :]
    cos = jnp.cos(ang)[None, :, None, :]
    sin = jnp.sin(ang)[None, :, None, :]
    xf = x.astype(jnp.float32)
    x1, x2 = xf[..., :half], xf[..., half:]
    return jnp.concatenate([x1 * cos - x2 * sin, x2 * cos + x1 * sin], axis=-1)


def retention(q, k, v, state0, chunk):
    B, L, H, _ = q.shape
    n = L // chunk
    lg = jnp.log(1.0 - jnp.exp2(-5.0 - jnp.arange(H, dtype=jnp.float32)))
    i = jnp.arange(chunk, dtype=jnp.float32)
    diff = i[:, None] - i[None, :]
    decay_in = jnp.where(diff >= 0, jnp.exp(jnp.maximum(diff, 0.0)[None] * lg[:, None, None]), 0.0)
    decay_q = jnp.exp((i[:, None] + 1.0) * lg[None, :])
    decay_k = jnp.exp((chunk - 1.0 - i)[:, None] * lg[None, :])
    decay_c = jnp.exp(chunk * lg)

    def to_chunks(a):
        return jnp.moveaxis(a.reshape(B, n, chunk, H, a.shape[-1]), 1, 0)

    def step(S, inp):
        qc, kc, vc = inp
        inner = jnp.einsum('bihd,bjhd->bhij', qc, kc) * decay_in[None]
        o = jnp.einsum('bhij,bjhe->bihe', inner, vc) + jnp.einsum('bihd,bhde->bihe', qc, S) * decay_q[None, :, :, None]
        S = S * decay_c[None, :, None, None] + jnp.einsum('bjhd,bjhe->bhde', kc * decay_k[None, :, :, None], vc)
        return S, o

    S, o = lax.scan(step, state0, (to_chunks(q), to_chunks(k), to_chunks(v)))
    return jnp.moveaxis(o, 0, 1).reshape(B, L, H, v.shape[-1]), S


def retention_heads(rq, rk, rv, pos, state0, chunk):
    B, L = rq.shape[:2]
    q = rotate(rq.reshape(B, L, RET_HEADS, RET_DK), pos)
    k = rotate(rk.reshape(B, L, RET_HEADS, RET_DK), pos) * (RET_DK ** -0.5)
    v = rv.reshape(B, L, RET_HEADS, RET_DV).astype(jnp.float32)
    return retention(q, k, v, state0, chunk)


def ret_output(o, g, gn_w, gn_b):
    B, L = o.shape[:2]
    mu = jnp.mean(o, axis=-1, keepdims=True)
    var = jnp.mean(jnp.square(o - mu), axis=-1, keepdims=True)
    y = ((o - mu) * lax.rsqrt(var + NORM_EPS)).reshape(B, L, RET_HEADS * RET_DV)
    y = y * gn_w.astype(jnp.float32) + gn_b.astype(jnp.float32)
    return (jax.nn.silu(g.astype(jnp.float32)) * y).astype(g.dtype)


def setup_inputs(seed: int = 0) -> dict:
    key = jax.random.key(seed)
    ks = jax.random.split(key, 32)
    n_pages = PAST_LEN // PAGE_SIZE
    n_pool = (5 * DEC_BATCH * n_pages + 3) // 4
    w_buf = min(WINDOW, PAST_LEN)
    f32 = jnp.float32

    def nrm(k, shape, scale):
        return jax.random.normal(k, shape, f32) * scale

    kv_row = (2, N_KV, HEAD_DIM)
    page_table = jax.random.permutation(ks[8], n_pool)[: DEC_BATCH * n_pages].reshape(DEC_BATCH, n_pages).astype(jnp.int32)
    return {
        "x_prompt": nrm(ks[0], (BATCH, SEQ, D_MODEL), 1.0),
        "x_sample": nrm(ks[1], (DEC_BATCH, DEC_SEQ, D_MODEL), 1.0),
        "c_prompt": nrm(ks[2], (BATCH, D_MODEL), 1.0),
        "c_sample": nrm(ks[3], (DEC_BATCH, D_MODEL), 1.0),
        "cache_cmp_kv": nrm(ks[4], (DEPTH, n_pool, PAGE_SIZE) + kv_row, 1.0),
        "cache_sel_kv": nrm(ks[5], (DEPTH, n_pool, PAGE_SIZE) + kv_row, 1.0),
        "state_win_kv": nrm(ks[6], (DEPTH, DEC_BATCH, w_buf) + kv_row, 1.0),
        "state_ret": nrm(ks[7], (DEPTH, DEC_BATCH, RET_HEADS, RET_DK, RET_DV), 0.5),
        "page_table": page_table,
        "rel_bias": nrm(ks[9], (N_BUCKETS, N_HEADS), 0.5),
        "w_mod": nrm(ks[10], (DEPTH, D_MODEL, 6 * D_MODEL), D_MODEL ** -0.5),
        "b_mod": nrm(ks[11], (DEPTH, 6 * D_MODEL), 0.02),
        "norm_mix": 1.0 + nrm(ks[12], (DEPTH, D_MODEL), 0.05),
        "norm_mlp": 1.0 + nrm(ks[13], (DEPTH, D_MODEL), 0.05),
        "w_in": nrm(ks[14], (DEPTH, D_MODEL, D_IN), D_MODEL ** -0.5),
        "cmp_pos": nrm(ks[15], (DEPTH, 2, CMP_BLOCK, HEAD_DIM), 0.5),
        "w_phi": nrm(ks[16], (DEPTH, 2, CMP_BLOCK, HEAD_DIM, HEAD_DIM), (CMP_BLOCK * HEAD_DIM) ** -0.5),
        "w_nsa_proj": nrm(ks[17], (DEPTH, N_HEADS * HEAD_DIM, D_MODEL), (N_HEADS * HEAD_DIM) ** -0.5),
        "ret_gn_w": 1.0 + nrm(ks[18], (DEPTH, RET_HEADS * RET_DV), 0.05),
        "ret_gn_b": nrm(ks[19], (DEPTH, RET_HEADS * RET_DV), 0.02),
        "w_ret_proj": nrm(ks[20], (DEPTH, RET_HEADS * RET_DV, D_MODEL), (RET_HEADS * RET_DV) ** -0.5),
        "w_out": nrm(ks[21], (DEPTH, D_MODEL, D_MODEL), D_MODEL ** -0.5),
        "w_up": nrm(ks[22], (DEPTH, D_MODEL, D_FF), D_MODEL ** -0.5),
        "w_down": nrm(ks[23], (DEPTH, D_FF, D_MODEL), D_FF ** -0.5),
        "norm_final": 1.0 + nrm(ks[24], (D_MODEL,), 0.05),
    }


def reference(x_prompt, x_sample, c_prompt, c_sample, cache_cmp_kv, cache_sel_kv, state_win_kv, state_ret, page_table,
              rel_bias, w_mod, b_mod, norm_mix, norm_mlp, w_in, cmp_pos, w_phi, w_nsa_proj, ret_gn_w, ret_gn_b,
              w_ret_proj, w_out, w_up, w_down, norm_final):

    def core_prompt(l, parts):
        q, kc_r, vc_r, ks_r, vs_r, kw_r, vw_r, ng, rq, rk, rv = parts[:11]
        B, S = q.shape[:2]
        kvh = lambda a: a.reshape(B, S, N_KV, HEAD_DIM)
        kc_r, vc_r, ks_r, vs_r, kw_r, vw_r = [kvh(a) for a in (kc_r, vc_r, ks_r, vs_r, kw_r, vw_r)]
        qn = q.reshape(B, S, N_HEADS, HEAD_DIM) * (HEAD_DIM ** -0.5)
        gates = jax.nn.sigmoid(ng).reshape(B, S, N_HEADS, 3)
        kc = compress(kc_r, cmp_pos[l, 0], w_phi[l, 0])
        vc = compress(vc_r, cmp_pos[l, 1], w_phi[l, 1])
        pad = ((0, 0), (WINDOW, 0), (0, 0), (0, 0))
        kw_p = jnp.pad(kw_r, pad)
        vw_p = jnp.pad(vw_r, pad)

        def q_block(i):
            q0 = i * Q_BLOCK
            sl = lambda a, n: lax.dynamic_slice_in_dim(a, q0, n, axis=1)
            t_pos = q0 + jnp.arange(Q_BLOCK, dtype=jnp.int32)
            return nsa_attend(sl(qn, Q_BLOCK), sl(gates, Q_BLOCK), t_pos, kc, vc, ks_r, vs_r,
                              sl(kw_p, Q_BLOCK + WINDOW), sl(vw_p, Q_BLOCK + WINDOW), q0 - WINDOW, rel_bias)

        o = lax.map(q_block, jnp.arange(S // Q_BLOCK, dtype=jnp.int32))
        o = jnp.moveaxis(o, 0, 1).reshape(B, S, N_HEADS * HEAD_DIM)
        state0 = jnp.zeros((B, RET_HEADS, RET_DK, RET_DV), jnp.float32)
        r, s_new = retention_heads(rq, rk, rv, jnp.arange(S, dtype=jnp.int32), state0, RET_CHUNK)
        win = jnp.stack([kw_r, vw_r], axis=2)[:, S - min(WINDOW, S):]
        return o, r, (jnp.stack([kc_r, vc_r], axis=2), jnp.stack([ks_r, vs_r], axis=2), win, s_new)

    def core_sample(l, parts):
        q, kc_r, vc_r, ks_r, vs_r, kw_r, vw_r, ng, rq, rk, rv = parts[:11]
        B, T = q.shape[:2]
        kvh = lambda a: a.reshape(B, T, N_KV, HEAD_DIM)
        qn = q.reshape(B, T, N_HEADS, HEAD_DIM) * (HEAD_DIM ** -0.5)
        gates = jax.nn.sigmoid(ng).reshape(B, T, N_HEADS, 3)
        t_pos = PAST_LEN + jnp.arange(T, dtype=jnp.int32)

        def past(cache):
            return cache[l][page_table].reshape(B, -1, 2, N_KV, HEAD_DIM)

        cmp_new = jnp.stack([kvh(kc_r), kvh(vc_r)], axis=2)
        sel_new = jnp.stack([kvh(ks_r), kvh(vs_r)], axis=2)
        win_new = jnp.stack([kvh(kw_r), kvh(vw_r)], axis=2)
        cmp_all = jnp.concatenate([past(cache_cmp_kv), cmp_new.astype(cache_cmp_kv.dtype)], axis=1)
        L = cmp_all.shape[1]
        n_cmp = L // CMP_BLOCK
        kc = compress(cmp_all[:, : n_cmp * CMP_BLOCK, 0], cmp_pos[l, 0], w_phi[l, 0])
        vc = compress(cmp_all[:, : n_cmp * CMP_BLOCK, 1], cmp_pos[l, 1], w_phi[l, 1])
        sel_all = jnp.concatenate([past(cache_sel_kv), sel_new.astype(cache_sel_kv.dtype)], axis=1)
        sel_all = jnp.pad(sel_all, ((0, 0), (0, (-L) % SEL_BLOCK), (0, 0), (0, 0), (0, 0)))
        win_all = jnp.concatenate([state_win_kv[l], win_new.astype(state_win_kv.dtype)], axis=1)
        w_buf = state_win_kv.shape[2]
        o = nsa_attend(qn, gates, t_pos, kc, vc, sel_all[:, :, 0], sel_all[:, :, 1],
                       win_all[:, :, 0], win_all[:, :, 1], PAST_LEN - w_buf, rel_bias)
        o = o.reshape(B, T, N_HEADS * HEAD_DIM)
        r, s_new = retention_heads(rq, rk, rv, t_pos, state_ret[l].astype(jnp.float32), T)
        return o, r, (cmp_new, sel_new, win_all[:, -w_buf:], s_new)

    def run_trunk(x, c, core):
        states = []
        for l in range(DEPTH):
            mod = jax.nn.silu(c) @ w_mod[l] + b_mod[l]
            sh1, sc1, g1, sh2, sc2, g2 = [m[:, None, :] for m in jnp.split(mod, 6, axis=-1)]
            h = rmsnorm(x, norm_mix[l]) * (1.0 + sc1) + sh1
            parts = split_in(h @ w_in[l])
            o_nsa, o_ret, st = core(l, parts)
            rg, ga, gb = parts[11], parts[12], parts[13]
            r = ret_output(o_ret, rg, ret_gn_w[l], ret_gn_b[l])
            merged = jax.nn.sigmoid(ga) * (o_nsa @ w_nsa_proj[l]) + jax.nn.sigmoid(gb) * (r @ w_ret_proj[l])
            x = x + g1 * (merged @ w_out[l])
            h2 = rmsnorm(x, norm_mlp[l]) * (1.0 + sc2) + sh2
            x = x + g2 * (jnp.square(jax.nn.relu(h2 @ w_up[l])) @ w_down[l])
            states.append(st)
        return rmsnorm(x, norm_final), states

    y_prompt, st_p = run_trunk(x_prompt, c_prompt, core_prompt)
    y_sample, st_s = run_trunk(x_sample, c_sample, core_sample)
    cmp_p, sel_p, win_p, ret_p = [jnp.stack([st[i] for st in st_p]) for i in range(4)]
    cmp_s, sel_s, win_s, ret_s = [jnp.stack([st[i] for st in st_s]) for i in range(4)]
    return (y_prompt, y_sample, cmp_p, sel_p, win_p, ret_p, cmp_s, sel_s, win_s, ret_s)
```

```python
import functools
import math

import jax
import jax.numpy as jnp
from jax import lax
from jax.experimental import pallas as pl
from jax.experimental.pallas import tpu as pltpu

D_MODEL = 2048
DEPTH = 4
PAST_LEN = 16384
PAGE_SIZE = 128
N_HEADS = 16
HEAD_DIM = 64
N_KV = 4
HPG = N_HEADS // N_KV
CMP_BLOCK = 32
SEL_BLOCK = 64
TOP_K = 16
WINDOW = 512
RET_HEADS = 4
RET_DK = 256
RET_DV = 512
RET_CHUNK = 128
ROPE_BASE = 10000.0
D_FF = 4 * D_MODEL
N_BUCKETS = 32
REL_MAX_DIST = 2048
NORM_EPS = 1e-6
FORCE_BONUS = 1e6
NEG = -1e30

KV_W = N_KV * HEAD_DIM
PAIR_W = 2 * KV_W
NG_SLOT = 16
NG_W = 128

LANES = 128
SUBLANES = 8
VMEM_LIMIT_BYTES = 56 * 1024 * 1024

QB = 128
KB = 128
N_BIAS_TILES = 14

f32 = jnp.float32
bf16 = jnp.bfloat16


def _cparams(sem):
    return pltpu.CompilerParams(dimension_semantics=sem, vmem_limit_bytes=VMEM_LIMIT_BYTES)


def _linear(name, xs, ws, extras, extra_specs, epilogue, out_shapes, out_specs, tm, tn):
    m = xs[0].shape[0]
    n = ws[0].shape[1]
    nx, ne = len(xs), len(extras)

    def body(*refs):
        x_refs, w_refs = refs[:nx], refs[nx:2 * nx]
        e_refs = refs[2 * nx:2 * nx + ne]
        o_refs = refs[2 * nx + ne:]
        accs = [jnp.dot(x[...].astype(bf16), w[...], preferred_element_type=f32)
                for x, w in zip(x_refs, w_refs)]
        epilogue(accs, e_refs, o_refs)

    in_specs = ([pl.BlockSpec((tm, x.shape[1]), lambda i, j: (i, 0)) for x in xs]
                + [pl.BlockSpec((w.shape[0], tn), lambda i, j: (0, j)) for w in ws]
                + list(extra_specs))
    return pl.pallas_call(
        body, grid=(m // tm, n // tn), in_specs=in_specs, out_specs=out_specs,
        out_shape=out_shapes, compiler_params=_cparams(("parallel", "arbitrary")),
        name=name)(*xs, *ws, *extras)


def _tile_spec(tm, tn, col0=0):
    return pl.BlockSpec((tm, tn), lambda i, j: (i, j + col0))


def _mod_spec(arr, tm, tn, rows_per_batch):
    if arr.ndim == 3:
        return pl.BlockSpec((None, 1, tn), lambda i, j: ((i * tm) // rows_per_batch, 0, j))
    return pl.BlockSpec((tm, tn), lambda i, j: (i, j))


def _mod_all(c_all, w_mod, b_mod):
    depth, d, n = w_mod.shape
    tn = 1024

    def body(c_ref, w_ref, b_ref, o_ref):
        c = c_ref[...]
        a = (c * jax.nn.sigmoid(c)).astype(bf16)
        o_ref[...] = jnp.dot(a, w_ref[...].astype(bf16), preferred_element_type=f32) + b_ref[...]

    return pl.pallas_call(
        body, grid=(depth, n // tn),
        in_specs=[pl.BlockSpec(c_all.shape, lambda l, j: (0, 0)),
                  pl.BlockSpec((None, d, tn), lambda l, j: (l, 0, j)),
                  pl.BlockSpec((None, 1, tn), lambda l, j: (l, 0, j))],
        out_specs=pl.BlockSpec((None, c_all.shape[0], tn), lambda l, j: (l, 0, j)),
        out_shape=jax.ShapeDtypeStruct((depth, c_all.shape[0], n), f32),
        compiler_params=_cparams(("parallel", "parallel")), name="mod_all",
    )(c_all, w_mod, b_mod.reshape(depth, 1, n))


def _norm(name, x, gain, sc, sh, rows_per_batch, out_dtype):
    m, d = x.shape
    tm = min(256, m)
    has_mod = sc is not None

    def body(*refs):
        if has_mod:
            x_ref, g_ref, sc_ref, sh_ref, o_ref = refs
        else:
            x_ref, g_ref, o_ref = refs
        xf = x_ref[...]
        y = xf * lax.rsqrt(jnp.mean(xf * xf, axis=-1, keepdims=True) + NORM_EPS) * g_ref[...]
        if has_mod:
            y = y * (1.0 + sc_ref[...]) + sh_ref[...]
        o_ref[...] = y.astype(out_dtype)

    in_specs = [pl.BlockSpec((tm, d), lambda i: (i, 0)), pl.BlockSpec((1, d), lambda i: (0, 0))]
    args = [x, gain.reshape(1, d)]
    if has_mod:
        for a in (sc, sh):
            if a.ndim == 3:
                in_specs.append(pl.BlockSpec((None, 1, d), lambda i: ((i * tm) // rows_per_batch, 0, 0)))
            else:
                in_specs.append(pl.BlockSpec((tm, d), lambda i: (i, 0)))
            args.append(a)
    return pl.pallas_call(
        body, grid=(m // tm,), in_specs=in_specs,
        out_specs=pl.BlockSpec((tm, d), lambda i: (i, 0)),
        out_shape=jax.ShapeDtypeStruct((m, d), out_dtype),
        compiler_params=_cparams(("parallel",)), name=name)(*args)


def _t5_bucket(dist):
    n = jnp.maximum(dist, 0)
    exact = N_BUCKETS // 2
    nf = jnp.maximum(n, exact).astype(f32)
    large = exact + (jnp.log(nf / exact) / math.log(REL_MAX_DIST / exact) * (N_BUCKETS - exact)).astype(jnp.int32)
    large = jnp.minimum(large, N_BUCKETS - 1)
    return jnp.where(n < exact, n, large)


def _bias_from_dist(dist, rel_bias):
    r, c = dist.shape
    tr = next(t for t in (512, 256, 128, 64, 32, 16, 8) if r % t == 0)

    def body(tab_ref, d_ref, o_ref):
        h = pl.program_id(0)
        bucket = _t5_bucket(d_ref[...])
        acc = jnp.zeros(bucket.shape, f32)
        for k in range(N_BUCKETS):
            acc = jnp.where(bucket == k, tab_ref[k, h], acc)
        o_ref[...] = acc

    return pl.pallas_call(
        body, grid=(N_HEADS, r // tr),
        in_specs=[pl.BlockSpec(memory_space=pltpu.SMEM),
                  pl.BlockSpec((tr, c), lambda h, i: (i, 0))],
        out_specs=pl.BlockSpec((None, tr, c), lambda h, i: (h, i, 0)),
        out_shape=jax.ShapeDtypeStruct((N_HEADS, r, c), f32),
        compiler_params=_cparams(("parallel", "parallel")), name="t5_bias",
    )(rel_bias, dist)


def _compress_accumulate(load_rows, pos_ref, wk_ref, wv_ref, rows):
    acc_k = jnp.zeros((rows, KV_W), f32)
    acc_v = jnp.zeros((rows, KV_W), f32)
    for c in range(CMP_BLOCK):
        lo = c * PAIR_W
        xk = (load_rows(c, 0) + pos_ref[:, lo:lo + KV_W]).astype(bf16)
        xv = (load_rows(c, 1) + pos_ref[:, lo + KV_W:lo + PAIR_W]).astype(bf16)
        acc_k = acc_k + jnp.dot(xk, wk_ref[c], preferred_element_type=f32)
        acc_v = acc_v + jnp.dot(xv, wv_ref[c], preferred_element_type=f32)
    return acc_k, acc_v


def _compress_prompt(kcvc, pos_row, wk, wv, batch):
    rows_total = kcvc.shape[0] // CMP_BLOCK
    n = rows_total // batch
    x = kcvc.reshape(rows_total, CMP_BLOCK * PAIR_W)

    def body(x_ref, pos_ref, wk_ref, wv_ref, kc_ref, vct_ref):
        def load_rows(c, kv):
            lo = c * PAIR_W + kv * KV_W
            return x_ref[:, lo:lo + KV_W]
        acc_k, acc_v = _compress_accumulate(load_rows, pos_ref, wk_ref, wv_ref, n)
        kc_ref[...] = acc_k.astype(bf16)
        vct_ref[...] = acc_v.T.astype(bf16)

    return pl.pallas_call(
        body, grid=(batch,),
        in_specs=[pl.BlockSpec((n, CMP_BLOCK * PAIR_W), lambda b: (b, 0)),
                  pl.BlockSpec(pos_row.shape, lambda b: (0, 0)),
                  pl.BlockSpec(wk.shape, lambda b: (0, 0, 0)),
                  pl.BlockSpec(wv.shape, lambda b: (0, 0, 0))],
        out_specs=[pl.BlockSpec((None, n, KV_W), lambda b: (b, 0, 0)),
                   pl.BlockSpec((None, KV_W, n), lambda b: (b, 0, 0))],
        out_shape=[jax.ShapeDtypeStruct((batch, n, KV_W), bf16),
                   jax.ShapeDtypeStruct((batch, KV_W, n), bf16)],
        compiler_params=_cparams(("parallel",)), name="compress_prompt",
    )(x, pos_row, wk, wv)


CMP_PAGES_PER_STEP = 32


def _compress_sample(cache8, layer, page_table, pos_row, wk, wv):
    batch, n_pages = page_table.shape
    blocks_per_page = PAGE_SIZE // CMP_BLOCK
    pp = CMP_PAGES_PER_STEP
    steps = n_pages // pp
    rows = pp * blocks_per_page
    rows2 = 2 * rows
    half_w = CMP_BLOCK * PAIR_W // 2
    half_c = CMP_BLOCK // 2

    def body(pt_ref, cache_ref, pos_ref, wk_ref, wv_ref, kc_ref, vct_ref, xbuf, sems, pair_scr):
        b, s = pl.program_id(0), pl.program_id(1)

        def page_copy(p):
            page = pt_ref[b, s * pp + p]
            return pltpu.make_async_copy(cache_ref.at[layer, page],
                                         xbuf.at[pl.ds(p * 2 * blocks_per_page, 2 * blocks_per_page), :],
                                         sems.at[p])
        for p in range(pp):
            page_copy(p).start()
        for p in range(pp):
            page_copy(p).wait()

        first_half = lax.broadcasted_iota(jnp.int32, (rows2, KV_W), 0) % 2 == 0
        results = []
        for kv, w_ref in enumerate((wk_ref, wv_ref)):
            acc = [jnp.zeros((rows2, KV_W), f32), jnp.zeros((rows2, KV_W), f32)]
            for cc in range(half_c):
                lo = cc * PAIR_W + kv * KV_W
                x = xbuf[:, lo:lo + KV_W]
                for hf in range(2):
                    plo = (cc + hf * half_c) * PAIR_W + kv * KV_W
                    xs = (x + pos_ref[:, plo:plo + KV_W]).astype(bf16)
                    acc[hf] = acc[hf] + jnp.dot(xs, w_ref[cc + hf * half_c], preferred_element_type=f32)
            y = jnp.where(first_half, acc[0], acc[1])
            halves = []
            for j in range(KV_W // LANES):
                pair_scr[...] = y[:, j * LANES:(j + 1) * LANES]
                halves.append(pair_scr[pl.ds(0, rows, stride=2), :] + pair_scr[pl.ds(1, rows, stride=2), :])
            results.append(jnp.concatenate(halves, axis=1))
        kc_ref[...] = results[0].astype(bf16)
        vct_ref[...] = results[1].T.astype(bf16)

    grid_spec = pltpu.PrefetchScalarGridSpec(
        num_scalar_prefetch=1, grid=(batch, steps),
        in_specs=[pl.BlockSpec(memory_space=pl.ANY),
                  pl.BlockSpec(pos_row.shape, lambda b, s, pt: (0, 0)),
                  pl.BlockSpec(wk.shape, lambda b, s, pt: (0, 0, 0)),
                  pl.BlockSpec(wv.shape, lambda b, s, pt: (0, 0, 0))],
        out_specs=[pl.BlockSpec((None, rows, KV_W), lambda b, s, pt: (b, s, 0)),
                   pl.BlockSpec((None, KV_W, rows), lambda b, s, pt: (b, 0, s))],
        scratch_shapes=[pltpu.VMEM((rows2, half_w), f32), pltpu.SemaphoreType.DMA((pp,)),
                        pltpu.VMEM((rows2, LANES), f32)])
    n_cmp = n_pages * blocks_per_page
    return pl.pallas_call(
        body, grid_spec=grid_spec,
        out_shape=[jax.ShapeDtypeStruct((batch, n_cmp, KV_W), bf16),
                   jax.ShapeDtypeStruct((batch, KV_W, n_cmp), bf16)],
        compiler_params=_cparams(("arbitrary", "arbitrary")), name="compress_sample",
    )(page_table, cache8, pos_row, wk, wv)


def _softmax_cols(s, mask):
    sm = jnp.where(mask, s, NEG)
    m = jnp.max(sm, axis=0, keepdims=True)
    e = jnp.exp(sm - m)
    p = e / jnp.sum(e, axis=0, keepdims=True)
    return jnp.where(mask, p, 0.0)


def _topk_mask(score, score_scr, n_rows, k):
    rows = lax.broadcasted_iota(jnp.int32, score.shape, 0)
    score_scr[...] = score

    def step(j, cnt):
        sj = score_scr[pl.ds(j, 1), :]
        ahead = (sj > score) | ((sj == score) & (j < rows))
        return cnt + jnp.where(ahead, 1.0, 0.0)

    cnt = lax.fori_loop(0, n_rows, step, jnp.zeros(score.shape, f32))
    return jnp.where(cnt < k, 1.0, 0.0)


def _nsa_prompt(q, kc, vct, ks, vst, kw, vwt, gates, bias_c, bias_t, batch, seq):
    nq = seq // QB
    n_cmp = seq // CMP_BLOCK
    n_sel = seq // SEL_BLOCK
    ratio = SEL_BLOCK // CMP_BLOCK
    k_eff = min(TOP_K, n_sel)
    wq = HPG * QB
    win_chunks = WINDOW // KB

    def body(q_ref, kc_ref, vct_ref, ks_ref, vst_ref, kw_ref, vwt_ref, g_ref, bc_ref, bt_ref,
             o_ref, imp_scr, score_scr, sel_scr, gate_scr):
        g = pl.program_id(1)
        i = pl.program_id(2)
        q0 = i * QB

        qt = q_ref[...].astype(f32).T
        qcat = jnp.concatenate([qt[h * HEAD_DIM:(h + 1) * HEAD_DIM] for h in range(HPG)], axis=1)
        qrep = jnp.concatenate([qcat] * N_KV, axis=0)
        row_grp = lax.broadcasted_iota(jnp.int32, (KV_W, wq), 0) // HEAD_DIM
        qpad = jnp.where(row_grp == g, qrep, 0.0).astype(bf16)

        t_row = q0 + lax.broadcasted_iota(jnp.int32, (1, wq), 1) % QB

        s_c = jnp.dot(kc_ref[...], qpad, preferred_element_type=f32) + bc_ref[...]
        c_end = lax.broadcasted_iota(jnp.int32, (n_cmp, wq), 0) * CMP_BLOCK + (CMP_BLOCK - 1)
        p_c = _softmax_cols(s_c, c_end <= t_row)
        o_c = jnp.dot(vct_ref[...], p_c.astype(bf16), preferred_element_type=f32)

        imp_scr[...] = sum(p_c[:, h * QB:(h + 1) * QB] for h in range(HPG))
        imp = sum(imp_scr[pl.ds(r, n_sel, stride=ratio), :] for r in range(ratio))
        blk = lax.broadcasted_iota(jnp.int32, (n_sel, QB), 0)
        t_q = q0 + lax.broadcasted_iota(jnp.int32, (n_sel, QB), 1)
        cur = t_q // SEL_BLOCK
        forced = (blk == 0) | (blk == cur) | (blk == cur - 1)
        score = imp + jnp.where(forced, FORCE_BONUS, 0.0)
        score = jnp.where(blk * SEL_BLOCK <= t_q, score, NEG)
        sel_scr[...] = _topk_mask(score, score_scr, n_sel, k_eff)

        kj = lax.broadcasted_iota(jnp.int32, (KB, QB), 0)
        qi = lax.broadcasted_iota(jnp.int32, (KB, QB), 1)
        rel = qi - kj
        upper = kj < SEL_BLOCK

        def attend_chunk(carry, k_ref, vt_ref, chunk, mask):
            m, l, acc = carry
            d_tiles = i - chunk
            k0 = pl.multiple_of(chunk * KB, KB)
            s = jnp.dot(k_ref[pl.ds(k0, KB), :], qpad, preferred_element_type=f32)
            penalty = jnp.where(mask, 0.0, NEG)
            s = s + bt_ref[jnp.minimum(d_tiles, N_BIAS_TILES - 1)] + jnp.concatenate([penalty] * HPG, axis=1)
            m_new = jnp.maximum(m, jnp.max(s, axis=0, keepdims=True))
            alpha = jnp.exp(m - m_new)
            p = jnp.exp(s - m_new)
            l = alpha * l + jnp.sum(p, axis=0, keepdims=True)
            acc = alpha * acc + jnp.dot(vt_ref[:, pl.ds(k0, KB)], p.astype(bf16),
                                        preferred_element_type=f32)
            return m_new, l, acc

        init = (jnp.full((1, wq), NEG, f32), jnp.zeros((1, wq), f32), jnp.zeros((HEAD_DIM, wq), f32))

        def sel_step(chunk, carry):
            rows2 = sel_scr[pl.ds(pl.multiple_of(chunk * (KB // SEL_BLOCK), KB // SEL_BLOCK),
                                  KB // SEL_BLOCK), :]
            chosen = jnp.where(upper, rows2[0:1, :], rows2[1:2, :]) > 0.5
            mask = chosen & (rel + (i - chunk) * KB >= 0)
            return attend_chunk(carry, ks_ref, vst_ref, chunk, mask)

        _, l_s, acc_s = lax.fori_loop(0, i + 1, sel_step, init)
        o_s = acc_s / l_s

        def win_step(dd, carry):
            chunk = i - dd
            dist = rel + dd * KB
            mask = (dist >= 0) & (dist < WINDOW)
            return attend_chunk(carry, kw_ref, vwt_ref, chunk, mask)

        _, l_w, acc_w = lax.fori_loop(0, jnp.minimum(i, win_chunks) + 1, win_step, init)
        o_w = acc_w / l_w

        gate_scr[...] = g_ref[...].T
        gg = gate_scr[pl.ds(pl.multiple_of(g * NG_SLOT, NG_SLOT), NG_SLOT), :]
        outs = []
        for h in range(HPG):
            sl = slice(h * QB, (h + 1) * QB)
            outs.append(gg[3 * h:3 * h + 1] * o_c[:, sl] + gg[3 * h + 1:3 * h + 2] * o_s[:, sl]
                        + gg[3 * h + 2:3 * h + 3] * o_w[:, sl])
        o_ref[...] = jnp.concatenate(outs, axis=0).T.astype(o_ref.dtype)

    return pl.pallas_call(
        body, grid=(batch, N_KV, nq),
        in_specs=[
            pl.BlockSpec((QB, KV_W), lambda b, g, i: (b * nq + i, g)),
            pl.BlockSpec((None, n_cmp, KV_W), lambda b, g, i: (b, 0, 0)),
            pl.BlockSpec((None, HEAD_DIM, n_cmp), lambda b, g, i: (b, g, 0)),
            pl.BlockSpec((seq, KV_W), lambda b, g, i: (b, 0)),
            pl.BlockSpec((HEAD_DIM, seq), lambda b, g, i: (g, b)),
            pl.BlockSpec((seq, KV_W), lambda b, g, i: (b, 0)),
            pl.BlockSpec((HEAD_DIM, seq), lambda b, g, i: (g, b)),
            pl.BlockSpec((QB, NG_W), lambda b, g, i: (b * nq + i, 0)),
            pl.BlockSpec((None, None, n_cmp, wq), lambda b, g, i: (i, g, 0, 0)),
            pl.BlockSpec((N_BIAS_TILES, None, KB, wq), lambda b, g, i: (0, g, 0, 0)),
        ],
        out_specs=pl.BlockSpec((QB, KV_W), lambda b, g, i: (b * nq + i, g)),
        out_shape=jax.ShapeDtypeStruct((batch * seq, N_HEADS * HEAD_DIM), bf16),
        scratch_shapes=[pltpu.VMEM((n_cmp, QB), f32), pltpu.VMEM((n_sel, QB), f32),
                        pltpu.VMEM((n_sel, QB), f32), pltpu.VMEM((NG_W, QB), f32)],
        compiler_params=_cparams(("parallel", "parallel", "arbitrary")), name="nsa_prompt",
    )(q, kc, vct, ks, vst, kw, vwt, gates, bias_c, bias_t)


def _retention(qk, v, rg, gn_w, gn_b, state0, dec_in, dec_q, dec_k, dec_c, batch, length, chunk, out_dtype):
    n = length // chunk

    def body(q_ref, k_ref, v_ref, rg_ref, gw_ref, gb_ref, s0_ref, din_ref, dq_ref, dk_ref, dc_ref,
             r_ref, s_ref):
        @pl.when(pl.program_id(2) == 0)
        def _():
            s_ref[...] = s0_ref[...]

        qc = q_ref[...].astype(bf16)
        kf = k_ref[...].astype(f32)
        vc = v_ref[...].astype(bf16)
        st = s_ref[...]
        inner = lax.dot_general(qc, kf.astype(bf16), (((1,), (1,)), ((), ())),
                                preferred_element_type=f32) * din_ref[...]
        o = jnp.dot(inner.astype(bf16), vc, preferred_element_type=f32)
        o = o + jnp.dot(qc, st.astype(bf16), preferred_element_type=f32) * dq_ref[...]
        kd = (kf * dk_ref[...]).astype(bf16)
        s_ref[...] = st * dc_ref[...] + lax.dot_general(kd, vc, (((0,), (0,)), ((), ())),
                                                        preferred_element_type=f32)
        mu = jnp.mean(o, axis=-1, keepdims=True)
        var = jnp.mean(jnp.square(o - mu), axis=-1, keepdims=True)
        y = (o - mu) * lax.rsqrt(var + NORM_EPS) * gw_ref[...] + gb_ref[...]
        gate = rg_ref[...]
        r_ref[...] = (gate * jax.nn.sigmoid(gate) * y).astype(r_ref.dtype)

    seq_spec = lambda w, col0=0: pl.BlockSpec((chunk, w), lambda b, h, c: (b * n + c, h + col0))
    head_spec = lambda shape: pl.BlockSpec((None,) + shape, lambda b, h, c: (h,) + (0,) * len(shape))
    return pl.pallas_call(
        body, grid=(batch, RET_HEADS, n),
        in_specs=[seq_spec(RET_DK), seq_spec(RET_DK, RET_HEADS), seq_spec(RET_DV), seq_spec(RET_DV),
                  pl.BlockSpec((1, RET_DV), lambda b, h, c: (0, h)),
                  pl.BlockSpec((1, RET_DV), lambda b, h, c: (0, h)),
                  pl.BlockSpec((None, None, RET_DK, RET_DV), lambda b, h, c: (b, h, 0, 0)),
                  head_spec((chunk, chunk)), head_spec((chunk, 1)), head_spec((chunk, 1)),
                  head_spec((1, 1))],
        out_specs=[seq_spec(RET_DV),
                   pl.BlockSpec((None, None, RET_DK, RET_DV), lambda b, h, c: (b, h, 0, 0))],
        out_shape=[jax.ShapeDtypeStruct((batch * length, RET_HEADS * RET_DV), out_dtype),
                   jax.ShapeDtypeStruct((batch, RET_HEADS, RET_DK, RET_DV), f32)],
        compiler_params=_cparams(("parallel", "parallel", "arbitrary")), name="retention",
    )(qk, qk, v, rg, gn_w, gn_b, state0, dec_in, dec_q, dec_k, dec_c)


def _retention_decays(chunk):
    lg = jnp.log(1.0 - jnp.exp2(-5.0 - jnp.arange(RET_HEADS, dtype=f32)))
    i = jnp.arange(chunk, dtype=f32)
    diff = i[:, None] - i[None, :]
    dec_in = jnp.where(diff >= 0, jnp.exp(jnp.maximum(diff, 0.0)[None] * lg[:, None, None]), 0.0)
    dec_q = jnp.exp((i[None, :] + 1.0) * lg[:, None])[..., None]
    dec_k = jnp.exp((chunk - 1.0 - i)[None, :] * lg[:, None])[..., None]
    dec_c = jnp.exp(chunk * lg)[:, None, None]
    return dec_in, dec_q, dec_k, dec_c


def _rope_tables(pos):
    half = RET_DK // 2
    inv = ROPE_BASE ** (-jnp.arange(half, dtype=f32) / half)
    ang = pos.astype(f32)[:, None] * inv[None, :]
    return jnp.cos(ang), jnp.sin(ang)


def _nsa_sample_cmp(kc, vct, qpt, bias_c, t_pos_row, n_sel, n_sel_pad):
    batch, n_cmp, _ = kc.shape
    ratio = SEL_BLOCK // CMP_BLOCK
    cols = qpt.shape[2]
    grp_cols = cols // HPG

    def body(kc_ref, vct_ref, q_ref, bc_ref, t_ref, oc_ref, sel_ref, imp_scr, score_scr):
        t_row = t_ref[...]
        s_c = jnp.dot(kc_ref[...], q_ref[...], preferred_element_type=f32) + bc_ref[...]
        c_end = lax.broadcasted_iota(jnp.int32, (n_cmp, cols), 0) * CMP_BLOCK + (CMP_BLOCK - 1)
        p_c = _softmax_cols(s_c, c_end <= t_row)
        oc_ref[...] = jnp.dot(vct_ref[...], p_c.astype(bf16), preferred_element_type=f32)
        imp_rep = p_c
        for h in range(1, HPG):
            imp_rep = imp_rep + pltpu.roll(p_c, h * grp_cols, axis=1)
        imp_scr[...] = jnp.zeros(imp_scr.shape, f32)
        imp_scr[pl.ds(0, n_cmp), :] = imp_rep
        imp = sum(imp_scr[pl.ds(r, n_sel_pad, stride=ratio), :] for r in range(ratio))
        blk = lax.broadcasted_iota(jnp.int32, (n_sel_pad, cols), 0)
        cur = t_row // SEL_BLOCK
        forced = (blk == 0) | (blk == cur) | (blk == cur - 1)
        score = imp + jnp.where(forced, FORCE_BONUS, 0.0)
        score = jnp.where(blk * SEL_BLOCK <= t_row, score, NEG)
        score = jnp.where(blk < n_sel, score, -jnp.inf)
        sel_ref[...] = _topk_mask(score, score_scr, n_sel, min(TOP_K, n_sel))

    return pl.pallas_call(
        body, grid=(batch,),
        in_specs=[pl.BlockSpec((None, n_cmp, KV_W), lambda b: (b, 0, 0)),
                  pl.BlockSpec((None, KV_W, n_cmp), lambda b: (b, 0, 0)),
                  pl.BlockSpec((None, KV_W, cols), lambda b: (b, 0, 0)),
                  pl.BlockSpec((n_cmp, cols), lambda b: (0, 0)),
                  pl.BlockSpec((1, cols), lambda b: (0, 0))],
        out_specs=[pl.BlockSpec((None, KV_W, cols), lambda b: (b, 0, 0)),
                   pl.BlockSpec((None, n_sel_pad, cols), lambda b: (b, 0, 0))],
        out_shape=[jax.ShapeDtypeStruct((batch, KV_W, cols), f32),
                   jax.ShapeDtypeStruct((batch, n_sel_pad, cols), f32)],
        scratch_shapes=[pltpu.VMEM((n_sel_pad * ratio, cols), f32), pltpu.VMEM((n_sel_pad, cols), f32)],
        compiler_params=_cparams(("parallel",)), name="nsa_sample_cmp",
    )(kc, vct, qpt, bias_c, t_pos_row)


SEL_PAGES_PER_STEP = 16


def _nsa_sample_sel(cache, layer, page_table, sel_new, win_all, qpt, sel_mask, oc_t, gates_t,
                    bias_s, bias_new, bias_w, t_pos_row):
    batch, n_pages = page_table.shape
    pp = SEL_PAGES_PER_STEP
    steps = n_pages // pp
    rows = pp * PAGE_SIZE
    cols = qpt.shape[2]
    blocks_per_page = PAGE_SIZE // SEL_BLOCK
    n_win_pad = win_all.shape[1]
    w_buf = min(WINDOW, PAST_LEN)

    def body(pt_ref, cache_ref, new_ref, win_ref, q_ref, selm_ref, oc_ref, g_ref, bs_ref, bn_ref, bw_ref,
             t_ref, o_ref, xbuf, sems, m_scr, l_scr, acc_scr):
        b, s = pl.program_id(0), pl.program_id(1)

        def page_copy(p):
            page = pt_ref[b, s * pp + p]
            return pltpu.make_async_copy(cache_ref.at[layer, page],
                                         xbuf.at[pl.ds(p * PAGE_SIZE, PAGE_SIZE), :], sems.at[p])
        for p in range(pp):
            page_copy(p).start()

        @pl.when(s == 0)
        def _():
            m_scr[...] = jnp.full(m_scr.shape, NEG, f32)
            l_scr[...] = jnp.zeros(l_scr.shape, f32)
            acc_scr[...] = jnp.zeros(acc_scr.shape, f32)

        qp = q_ref[...]
        t_row = t_ref[...]

        def online(kv, bias, mask):
            s_t = jnp.dot(kv[:, :KV_W].astype(bf16), qp, preferred_element_type=f32) + bias
            s_t = jnp.where(mask, s_t, NEG)
            m_old = m_scr[...]
            m_new = jnp.maximum(m_old, jnp.max(s_t, axis=0, keepdims=True))
            alpha = jnp.exp(m_old - m_new)
            p = jnp.where(mask, jnp.exp(s_t - m_new), 0.0)
            l_scr[...] = alpha * l_scr[...] + jnp.sum(p, axis=0, keepdims=True)
            acc_scr[...] = alpha * acc_scr[...] + lax.dot_general(
                kv[:, KV_W:].astype(bf16), p.astype(bf16), (((0,), (0,)), ((), ())),
                preferred_element_type=f32)
            m_scr[...] = m_new

        def expand(mask_rows, n_keys):
            n_blocks = mask_rows.shape[0]
            key_blk = lax.broadcasted_iota(jnp.int32, (n_keys, n_blocks), 0) // SEL_BLOCK
            onehot = jnp.where(key_blk == lax.broadcasted_iota(jnp.int32, (n_keys, n_blocks), 1), 1.0, 0.0)
            return jnp.dot(onehot, mask_rows, preferred_element_type=f32)

        for p in range(pp):
            page_copy(p).wait()
        nb = pp * blocks_per_page
        chosen = expand(selm_ref[pl.ds(pl.multiple_of(s * nb, nb), nb), :], rows) > 0.5
        online(xbuf[...], bs_ref[...], chosen)

        @pl.when(s == steps - 1)
        def _():
            chosen_n = expand(selm_ref[pl.ds(n_pages * blocks_per_page, SUBLANES), :], PAGE_SIZE) > 0.5
            pos_n = PAST_LEN + lax.broadcasted_iota(jnp.int32, (PAGE_SIZE, cols), 0)
            online(new_ref[...], bn_ref[...], chosen_n & (pos_n <= t_row))
            o_s = acc_scr[...] / l_scr[...]

            kvw = win_ref[...]
            kpos = (PAST_LEN - w_buf) + lax.broadcasted_iota(jnp.int32, (n_win_pad, cols), 0)
            dist = t_row - kpos
            mask_w = (dist >= 0) & (dist < WINDOW) & (kpos >= 0)
            s_w = jnp.dot(kvw[:, :KV_W].astype(bf16), qp, preferred_element_type=f32) + bw_ref[...]
            p_w = _softmax_cols(s_w, mask_w)
            o_w = lax.dot_general(kvw[:, KV_W:].astype(bf16), p_w.astype(bf16),
                                  (((0,), (0,)), ((), ())), preferred_element_type=f32)
            gts = g_ref[...]
            o_ref[...] = gts[0:1] * oc_ref[...] + gts[1:2] * o_s + gts[2:3] * o_w

    grid_spec = pltpu.PrefetchScalarGridSpec(
        num_scalar_prefetch=1, grid=(batch, steps),
        in_specs=[pl.BlockSpec(memory_space=pl.ANY),
                  pl.BlockSpec((None, PAGE_SIZE, PAIR_W), lambda b, s, pt: (b, 0, 0)),
                  pl.BlockSpec((None, n_win_pad, PAIR_W), lambda b, s, pt: (b, 0, 0)),
                  pl.BlockSpec((None, KV_W, cols), lambda b, s, pt: (b, 0, 0)),
                  pl.BlockSpec((None, sel_mask.shape[1], cols), lambda b, s, pt: (b, 0, 0)),
                  pl.BlockSpec((None, KV_W, cols), lambda b, s, pt: (b, 0, 0)),
                  pl.BlockSpec((None, SUBLANES, cols), lambda b, s, pt: (b, 0, 0)),
                  pl.BlockSpec((rows, cols), lambda b, s, pt: (s, 0)),
                  pl.BlockSpec((PAGE_SIZE, cols), lambda b, s, pt: (0, 0)),
                  pl.BlockSpec((n_win_pad, cols), lambda b, s, pt: (0, 0)),
                  pl.BlockSpec((1, cols), lambda b, s, pt: (0, 0))],
        out_specs=pl.BlockSpec((None, KV_W, cols), lambda b, s, pt: (b, 0, 0)),
        scratch_shapes=[pltpu.VMEM((rows, PAIR_W), f32), pltpu.SemaphoreType.DMA((pp,)),
                        pltpu.VMEM((1, cols), f32), pltpu.VMEM((1, cols), f32),
                        pltpu.VMEM((KV_W, cols), f32)])
    return pl.pallas_call(
        body, grid_spec=grid_spec,
        out_shape=jax.ShapeDtypeStruct((batch, KV_W, cols), f32),
        compiler_params=_cparams(("arbitrary", "arbitrary")), name="nsa_sample_sel",
    )(page_table, cache, sel_new, win_all, qpt, sel_mask, oc_t, gates_t, bias_s, bias_new, bias_w,
      t_pos_row)


_OFF_Q = 0
_OFF_KV = N_HEADS * HEAD_DIM
_OFF_NG = _OFF_KV + 6 * KV_W
_OFF_RQ = _OFF_NG + 3 * N_HEADS
_OFF_RV = _OFF_RQ + 2 * RET_HEADS * RET_DK
_OFF_RG = _OFF_RV + RET_HEADS * RET_DV
_D_IN = _OFF_RG + RET_HEADS * RET_DV + 2 * D_MODEL


def _prep_w_in(w_in_l):
    w = w_in_l.astype(bf16)
    w_attn = w[:, :_OFF_NG]
    ng = w[:, _OFF_NG:_OFF_RQ].reshape(D_MODEL, N_KV, 3 * HPG)
    w_ng = jnp.pad(ng, ((0, 0), (0, 0), (0, NG_SLOT - 3 * HPG))).reshape(D_MODEL, N_KV * NG_SLOT)
    w_ng = jnp.pad(w_ng, ((0, 0), (0, NG_W - N_KV * NG_SLOT)))
    w_ret = w[:, _OFF_RQ:]
    return w_attn, w_ng, w_ret


def _prep_w_phi(w_phi_l, cmp_pos_l):
    eye = jnp.eye(N_KV, dtype=f32)
    wk = jnp.einsum('gh,cde->cgdhe', eye, w_phi_l[0]).reshape(CMP_BLOCK, KV_W, KV_W).astype(bf16)
    wv = jnp.einsum('gh,cde->cgdhe', eye, w_phi_l[1]).reshape(CMP_BLOCK, KV_W, KV_W).astype(bf16)
    pos = jnp.broadcast_to(cmp_pos_l[:, :, None, :], (2, CMP_BLOCK, N_KV, HEAD_DIM))
    pos_row = jnp.transpose(pos, (1, 0, 2, 3)).reshape(1, CMP_BLOCK * PAIR_W)
    return wk, wv, pos_row


def _layer(x, mod, lw, rows_per_batch, tm, act, rope, attend):
    m = x.shape[0]
    sh1, sc1, g1, sh2, sc2, g2 = mod
    h = _norm("norm_mix", x, lw["norm_mix"], sc1, sh1, rows_per_batch, act)

    def plain(dtype, scale=1.0):
        def epi(accs, e_refs, o_refs):
            o_refs[0][...] = (accs[0] * scale).astype(dtype)
        return epi

    w_attn, w_ng, w_ret = lw["w_attn"], lw["w_ng"], lw["w_ret"]
    tn = 512
    q = _linear("proj_q", [h], [w_attn[:, :_OFF_KV]], [], [], plain(act, HEAD_DIM ** -0.5),
                jax.ShapeDtypeStruct((m, _OFF_KV), act), _tile_spec(tm, tn), tm, tn)
    kv_raw = _linear("proj_kv", [h], [w_attn[:, _OFF_KV:]], [], [], plain(f32),
                     jax.ShapeDtypeStruct((m, 6 * KV_W), f32), _tile_spec(tm, tn), tm, tn)

    def sig_epi(accs, e_refs, o_refs):
        o_refs[0][...] = jax.nn.sigmoid(accs[0])
    gates = _linear("proj_ng", [h], [w_ng], [], [], sig_epi,
                    jax.ShapeDtypeStruct((m, NG_W), f32), _tile_spec(tm, NG_W), tm, NG_W)

    cos, sin, rope_spec = rope
    n_qk_tiles = RET_HEADS * RET_DK // tn

    def rot_epi(accs, e_refs, o_refs):
        a = accs[0]
        c, s = e_refs[0][...], e_refs[1][...]
        scale = jnp.where(pl.program_id(1) >= n_qk_tiles, RET_DK ** -0.5, 1.0)
        half = RET_DK // 2
        outs = []
        for hh in range(tn // RET_DK):
            x1 = a[:, hh * RET_DK:hh * RET_DK + half]
            x2 = a[:, hh * RET_DK + half:(hh + 1) * RET_DK]
            outs += [x1 * c - x2 * s, x2 * c + x1 * s]
        o_refs[0][...] = (jnp.concatenate(outs, axis=1) * scale).astype(act)

    n_rot = 2 * RET_HEADS * RET_DK
    rqk = _linear("proj_rqk", [h], [w_ret[:, :n_rot]], [cos, sin], [rope_spec, rope_spec], rot_epi,
                  jax.ShapeDtypeStruct((m, n_rot), act), _tile_spec(tm, tn), tm, tn)
    n_rv = RET_HEADS * RET_DV
    rv = _linear("proj_rv", [h], [w_ret[:, n_rot:n_rot + n_rv]], [], [], plain(act),
                 jax.ShapeDtypeStruct((m, n_rv), act), _tile_spec(tm, tn), tm, tn)
    gts = _linear("proj_gates", [h], [w_ret[:, n_rot + n_rv:]], [], [], plain(f32),
                  jax.ShapeDtypeStruct((m, 3 * D_MODEL), f32), _tile_spec(tm, tn), tm, tn)

    o_nsa, r, states = attend(q, kv_raw, gates, rqk, rv, gts)

    def merge_epi(accs, e_refs, o_refs):
        o_refs[0][...] = (jax.nn.sigmoid(e_refs[0][...]) * accs[0]
                          + jax.nn.sigmoid(e_refs[1][...]) * accs[1]).astype(act)
    nt = D_MODEL // tn
    merged = _linear("merge", [o_nsa, r], [lw["w_nsa_proj"], lw["w_ret_proj"]], [gts, gts],
                     [_tile_spec(tm, tn, nt), _tile_spec(tm, tn, 2 * nt)], merge_epi,
                     jax.ShapeDtypeStruct((m, D_MODEL), act), _tile_spec(tm, tn), tm, tn)

    def resid_epi(accs, e_refs, o_refs):
        o_refs[0][...] = e_refs[0][...] + e_refs[1][...] * accs[0]
    x = _linear("out_proj", [merged], [lw["w_out"]], [x, g1],
                [_tile_spec(tm, tn), _mod_spec(g1, tm, tn, rows_per_batch)], resid_epi,
                jax.ShapeDtypeStruct((m, D_MODEL), f32), _tile_spec(tm, tn), tm, tn)

    h2 = _norm("norm_mlp", x, lw["norm_mlp"], sc2, sh2, rows_per_batch, act)

    def up_epi(accs, e_refs, o_refs):
        o_refs[0][...] = jnp.square(jnp.maximum(accs[0], 0.0)).astype(act)
    u = _linear("mlp_up", [h2], [lw["w_up"]], [], [], up_epi,
                jax.ShapeDtypeStruct((m, D_FF), act), _tile_spec(tm, tn), tm, tn)
    tm_d = min(tm, 512)
    x = _linear("mlp_down", [u], [lw["w_down"]], [x, g2],
                [_tile_spec(tm_d, tn), _mod_spec(g2, tm_d, tn, rows_per_batch)], resid_epi,
                jax.ShapeDtypeStruct((m, D_MODEL), f32), _tile_spec(tm_d, tn), tm_d, tn)
    return x, states


def kernel(x_prompt, x_sample, c_prompt, c_sample, cache_cmp_kv, cache_sel_kv, state_win_kv, state_ret,
           page_table, rel_bias, w_mod, b_mod, norm_mix, norm_mlp, w_in, cmp_pos, w_phi, w_nsa_proj,
           ret_gn_w, ret_gn_b, w_ret_proj, w_out, w_up, w_down, norm_final):
    batch, seq, _ = x_prompt.shape
    dbatch, dseq, _ = x_sample.shape
    depth = w_in.shape[0]
    n_pool = cache_cmp_kv.shape[1]
    n_pages = page_table.shape[1]
    i32 = jnp.int32

    n_c = batch + dbatch
    c_all = jnp.pad(jnp.concatenate([c_prompt, c_sample], axis=0), ((0, (-n_c) % SUBLANES), (0, 0)))
    mod_all = _mod_all(c_all, w_mod, b_mod)

    nq = seq // QB
    n_cmp_p = seq // CMP_BLOCK
    kj = jnp.arange(KB, dtype=i32)[:, None]
    qi = jnp.arange(QB, dtype=i32)[None, :]
    dist_tiles = (jnp.arange(N_BIAS_TILES, dtype=i32)[:, None, None] * KB + qi[None] - kj[None])
    bias_t = _bias_from_dist(dist_tiles.reshape(N_BIAS_TILES * KB, QB), rel_bias)
    bias_t = bias_t.reshape(N_KV, HPG, N_BIAS_TILES, KB, QB).transpose(2, 0, 3, 1, 4).reshape(
        N_BIAS_TILES, N_KV, KB, HPG * QB)
    t_all = jnp.arange(seq, dtype=i32)
    c_end_p = jnp.arange(n_cmp_p, dtype=i32) * CMP_BLOCK + (CMP_BLOCK - 1)
    bias_c = _bias_from_dist(t_all[None, :] - c_end_p[:, None], rel_bias)
    bias_c = bias_c.reshape(N_KV, HPG, n_cmp_p, nq, QB).transpose(3, 0, 2, 1, 4).reshape(
        nq, N_KV, n_cmp_p, HPG * QB)

    cos_p, sin_p = _rope_tables(jnp.arange(seq, dtype=i32))
    dec_p = _retention_decays(RET_CHUNK)

    cols = HPG * N_KV * dseq
    t_cols = PAST_LEN + jnp.tile(jnp.arange(dseq, dtype=i32), HPG * N_KV)
    head_cols = jnp.tile(jnp.repeat(jnp.arange(N_KV, dtype=i32), dseq), HPG) * HPG + jnp.repeat(
        jnp.arange(HPG, dtype=i32), N_KV * dseq)
    t_new = PAST_LEN + jnp.arange(dseq, dtype=i32)

    def col_bias(key_pos):
        per_head = _bias_from_dist(t_new[:, None] - key_pos[None, :], rel_bias)
        t_idx = jnp.tile(jnp.arange(dseq, dtype=i32), HPG * N_KV)
        return per_head[head_cols, t_idx, :].T

    n_cmp_s = PAST_LEN // CMP_BLOCK
    bias_c_s = col_bias(jnp.arange(n_cmp_s, dtype=i32) * CMP_BLOCK + (CMP_BLOCK - 1))
    bias_s_s = col_bias(jnp.arange(PAST_LEN, dtype=i32))
    bias_new_s = col_bias(PAST_LEN + jnp.arange(PAGE_SIZE, dtype=i32))
    w_buf = state_win_kv.shape[2]
    n_win_pad = -(-(w_buf + dseq) // LANES) * LANES
    bias_w_s = col_bias(PAST_LEN - w_buf + jnp.arange(n_win_pad, dtype=i32))
    n_sel_s = -(-(PAST_LEN + dseq) // SEL_BLOCK)
    n_sel_pad = -(-(n_pages * (PAGE_SIZE // SEL_BLOCK) + PAGE_SIZE // SEL_BLOCK) // SUBLANES) * SUBLANES
    assert n_sel_pad >= n_sel_s
    cos_s, sin_s = _rope_tables(t_new)
    cos_s, sin_s = jnp.tile(cos_s, (dbatch, 1)), jnp.tile(sin_s, (dbatch, 1))
    dec_s = _retention_decays(dseq)
    t_pos_row = t_cols.reshape(1, cols)

    cache_cmp8 = cache_cmp_kv.reshape(depth, n_pool, 2 * (PAGE_SIZE // CMP_BLOCK), CMP_BLOCK * PAIR_W // 2)
    cache_sel2 = cache_sel_kv.reshape(depth, n_pool, PAGE_SIZE, PAIR_W)

    m_p = batch * seq
    m_s = dbatch * dseq
    tm_p = 1024
    tm_s = m_s

    xp = x_prompt.reshape(m_p, D_MODEL)
    xs = x_sample.reshape(m_s, D_MODEL)
    st_p, st_s = [], []
    for l in range(depth):
        w_attn, w_ng, w_ret = _prep_w_in(w_in[l])
        wk, wv, pos_row = _prep_w_phi(w_phi[l], cmp_pos[l])
        lw = dict(norm_mix=norm_mix[l], norm_mlp=norm_mlp[l], w_attn=w_attn, w_ng=w_ng, w_ret=w_ret,
                  w_nsa_proj=w_nsa_proj[l].astype(bf16), w_ret_proj=w_ret_proj[l].astype(bf16),
                  w_out=w_out[l].astype(bf16), w_up=w_up[l].astype(bf16), w_down=w_down[l].astype(bf16))
        gn_w = ret_gn_w[l].reshape(1, -1)
        gn_b = ret_gn_b[l].reshape(1, -1)

        mod_p = [a.reshape(batch, 1, D_MODEL) for a in jnp.split(mod_all[l, :batch], 6, axis=-1)]

        def attend_prompt(q, kv_raw, gates, rqk, rv, gts):
            kcvc, ksvs, kwvw = (kv_raw[:, j * PAIR_W:(j + 1) * PAIR_W] for j in range(3))
            kc, vct = _compress_prompt(kcvc, pos_row, wk, wv, batch)
            ks, kw = ksvs[:, :KV_W].astype(bf16), kwvw[:, :KV_W].astype(bf16)
            vst, vwt = ksvs[:, KV_W:].astype(bf16).T, kwvw[:, KV_W:].astype(bf16).T
            o = _nsa_prompt(q, kc, vct, ks, vst, kw, vwt, gates, bias_c, bias_t, batch, seq)
            zero_state = jnp.zeros((batch, RET_HEADS, RET_DK, RET_DV), f32)
            r, s_new = _retention(rqk, rv, gts, gn_w, gn_b, zero_state, *dec_p, batch, seq, RET_CHUNK, bf16)
            kv6 = lambda a: a.reshape(batch, seq, 2, N_KV, HEAD_DIM)
            win = kv6(kwvw)[:, seq - min(WINDOW, seq):]
            return o, r, (kv6(kcvc), kv6(ksvs), win, s_new)

        rope_p = (cos_p, sin_p, pl.BlockSpec((tm_p, RET_DK // 2), lambda i, j: (i % (seq // tm_p), 0)))
        xp, st = _layer(xp, mod_p, lw, seq, tm_p, bf16, rope_p, attend_prompt)
        st_p.append(st)

        mod_rows = jnp.repeat(mod_all[l, batch:batch + dbatch], dseq, axis=0)
        mod_s = jnp.split(mod_rows, 6, axis=-1)

        def attend_sample(q, kv_raw, gates, rqk, rv, gts):
            kv5 = kv_raw.reshape(dbatch, dseq, 3, PAIR_W)
            cmp_new, sel_new, win_new = kv5[:, :, 0], kv5[:, :, 1], kv5[:, :, 2]
            kc, vct = _compress_sample(cache_cmp8, l, page_table, pos_row, wk, wv)
            q5 = q.reshape(dbatch, dseq, N_KV, HPG, HEAD_DIM)
            qpt = jnp.einsum('btghd,gk->bkdhgt', q5, jnp.eye(N_KV, dtype=q.dtype)).reshape(
                dbatch, KV_W, cols).astype(bf16)
            oc_t, sel_mask = _nsa_sample_cmp(kc, vct, qpt, bias_c_s, t_pos_row, n_sel_s, n_sel_pad)
            g4 = gates[:, :N_KV * NG_SLOT].reshape(dbatch, dseq, N_KV, NG_SLOT)[..., :3 * HPG]
            g4 = g4.reshape(dbatch, dseq, N_KV, HPG, 3)
            gates_t = jnp.transpose(g4, (0, 4, 3, 2, 1)).reshape(dbatch, 3, cols)
            gates_t = jnp.pad(gates_t, ((0, 0), (0, SUBLANES - 3), (0, 0)))
            sel_pad = jnp.pad(sel_new, ((0, 0), (0, PAGE_SIZE - dseq), (0, 0)))
            win_all = jnp.concatenate([state_win_kv[l].reshape(dbatch, w_buf, PAIR_W), win_new], axis=1)
            win_pad = jnp.pad(win_all, ((0, 0), (0, n_win_pad - w_buf - dseq), (0, 0)))
            o_t = _nsa_sample_sel(cache_sel2, l, page_table, sel_pad, win_pad, qpt, sel_mask, oc_t,
                                  gates_t, bias_s_s, bias_new_s, bias_w_s, t_pos_row)
            o6 = o_t.reshape(dbatch, N_KV, HEAD_DIM, HPG, N_KV, dseq)
            o = jnp.einsum('bgdhgt->btghd', o6).reshape(m_s, N_HEADS * HEAD_DIM)
            r, s_new = _retention(rqk, rv, gts, gn_w, gn_b, state_ret[l], *dec_s, dbatch, dseq, dseq, f32)
            kv6 = lambda a: a.reshape(dbatch, -1, 2, N_KV, HEAD_DIM)
            return o, r, (kv6(cmp_new), kv6(sel_new), kv6(win_all[:, -w_buf:]), s_new)

        rope_s = (cos_s, sin_s, pl.BlockSpec((tm_s, RET_DK // 2), lambda i, j: (i, 0)))
        xs, st = _layer(xs, mod_s, lw, dseq, tm_s, f32, rope_s, attend_sample)
        st_s.append(st)

    y_p = _norm("norm_final", xp, norm_final, None, None, seq, f32).reshape(batch, seq, D_MODEL)
    y_s = _norm("norm_final", xs, norm_final, None, None, dseq, f32).reshape(dbatch, dseq, D_MODEL)
    cmp_p, sel_p, win_p, ret_p = [jnp.stack([st[i] for st in st_p]) for i in range(4)]
    cmp_s, sel_s, win_s, ret_s = [jnp.stack([st[i] for st in st_s]) for i in range(4)]
    return (y_p, y_s, cmp_p, sel_p, win_p, ret_p, cmp_s, sel_s, win_s, ret_s)
```

```python
import functools
import math

import jax
import jax.numpy as jnp
from jax import lax
from jax.experimental import pallas as pl
from jax.experimental.pallas import tpu as pltpu

D_MODEL = 2048
DEPTH = 4
PAST_LEN = 16384
PAGE_SIZE = 128
N_HEADS = 16
HEAD_DIM = 64
N_KV = 4
HPG = N_HEADS // N_KV
CMP_BLOCK = 32
SEL_BLOCK = 64
TOP_K = 16
WINDOW = 512
RET_HEADS = 4
RET_DK = 256
RET_DV = 512
RET_CHUNK = 128
ROPE_BASE = 10000.0
D_FF = 4 * D_MODEL
N_BUCKETS = 32
REL_MAX_DIST = 2048
NORM_EPS = 1e-6
FORCE_BONUS = 1e6
NEG = -1e30

KV_W = N_KV * HEAD_DIM
PAIR_W = 2 * KV_W
NG_SLOT = 16
NG_W = 128

LANES = 128
SUBLANES = 8
VMEM_LIMIT_BYTES = 56 * 1024 * 1024

QB = 128
KB = 128
N_BIAS_TILES = 14
SEL_SUB = 4

f32 = jnp.float32
bf16 = jnp.bfloat16


def _cparams(sem):
    return pltpu.CompilerParams(dimension_semantics=sem, vmem_limit_bytes=VMEM_LIMIT_BYTES)


def _linear(name, xs, ws, extras, extra_specs, epilogue, out_shapes, out_specs, tm, tn):
    m = xs[0].shape[0]
    n = ws[0].shape[1]
    nx, ne = len(xs), len(extras)

    def body(*refs):
        x_refs, w_refs = refs[:nx], refs[nx:2 * nx]
        e_refs = refs[2 * nx:2 * nx + ne]
        o_refs = refs[2 * nx + ne:]
        accs = [jnp.dot(x[...].astype(bf16), w[...], preferred_element_type=f32)
                for x, w in zip(x_refs, w_refs)]
        epilogue(accs, e_refs, o_refs)

    in_specs = ([pl.BlockSpec((tm, x.shape[1]), lambda i, j: (i, 0)) for x in xs]
                + [pl.BlockSpec((w.shape[0], tn), lambda i, j: (0, j)) for w in ws]
                + list(extra_specs))
    return pl.pallas_call(
        body, grid=(m // tm, n // tn), in_specs=in_specs, out_specs=out_specs,
        out_shape=out_shapes, compiler_params=_cparams(("parallel", "arbitrary")),
        name=name)(*xs, *ws, *extras)


def _tile_spec(tm, tn, col0=0):
    return pl.BlockSpec((tm, tn), lambda i, j: (i, j + col0))


def _mod_spec(arr, tm, tn, rows_per_batch):
    if arr.ndim == 3:
        return pl.BlockSpec((None, 1, tn), lambda i, j: ((i * tm) // rows_per_batch, 0, j))
    return pl.BlockSpec((tm, tn), lambda i, j: (i, j))


def _mod_all(c_all, w_mod, b_mod):
    depth, d, n = w_mod.shape
    tn = 1024

    def body(c_ref, w_ref, b_ref, o_ref):
        c = c_ref[...]
        a = (c * jax.nn.sigmoid(c)).astype(bf16)
        o_ref[...] = jnp.dot(a, w_ref[...].astype(bf16), preferred_element_type=f32) + b_ref[...]

    return pl.pallas_call(
        body, grid=(depth, n // tn),
        in_specs=[pl.BlockSpec(c_all.shape, lambda l, j: (0, 0)),
                  pl.BlockSpec((None, d, tn), lambda l, j: (l, 0, j)),
                  pl.BlockSpec((None, 1, tn), lambda l, j: (l, 0, j))],
        out_specs=pl.BlockSpec((None, c_all.shape[0], tn), lambda l, j: (l, 0, j)),
        out_shape=jax.ShapeDtypeStruct((depth, c_all.shape[0], n), f32),
        compiler_params=_cparams(("parallel", "parallel")), name="mod_all",
    )(c_all, w_mod, b_mod.reshape(depth, 1, n))


def _norm(name, x, gain, sc, sh, rows_per_batch, out_dtype):
    m, d = x.shape
    tm = min(256, m)
    has_mod = sc is not None

    def body(*refs):
        if has_mod:
            x_ref, g_ref, sc_ref, sh_ref, o_ref = refs
        else:
            x_ref, g_ref, o_ref = refs
        xf = x_ref[...]
        y = xf * lax.rsqrt(jnp.mean(xf * xf, axis=-1, keepdims=True) + NORM_EPS) * g_ref[...]
        if has_mod:
            y = y * (1.0 + sc_ref[...]) + sh_ref[...]
        o_ref[...] = y.astype(out_dtype)

    in_specs = [pl.BlockSpec((tm, d), lambda i: (i, 0)), pl.BlockSpec((1, d), lambda i: (0, 0))]
    args = [x, gain.reshape(1, d)]
    if has_mod:
        for a in (sc, sh):
            if a.ndim == 3:
                in_specs.append(pl.BlockSpec((None, 1, d), lambda i: ((i * tm) // rows_per_batch, 0, 0)))
            else:
                in_specs.append(pl.BlockSpec((tm, d), lambda i: (i, 0)))
            args.append(a)
    return pl.pallas_call(
        body, grid=(m // tm,), in_specs=in_specs,
        out_specs=pl.BlockSpec((tm, d), lambda i: (i, 0)),
        out_shape=jax.ShapeDtypeStruct((m, d), out_dtype),
        compiler_params=_cparams(("parallel",)), name=name)(*args)


def _t5_bucket(dist):
    n = jnp.maximum(dist, 0)
    exact = N_BUCKETS // 2
    nf = jnp.maximum(n, exact).astype(f32)
    large = exact + (jnp.log(nf / exact) / math.log(REL_MAX_DIST / exact) * (N_BUCKETS - exact)).astype(jnp.int32)
    large = jnp.minimum(large, N_BUCKETS - 1)
    return jnp.where(n < exact, n, large)


def _bias_from_dist(dist, rel_bias):
    r, c = dist.shape
    tr = next(t for t in (512, 256, 128, 64, 32, 16, 8) if r % t == 0)

    def body(tab_ref, d_ref, o_ref):
        h = pl.program_id(0)
        bucket = _t5_bucket(d_ref[...])
        acc = jnp.zeros(bucket.shape, f32)
        for k in range(N_BUCKETS):
            acc = jnp.where(bucket == k, tab_ref[k, h], acc)
        o_ref[...] = acc

    return pl.pallas_call(
        body, grid=(N_HEADS, r // tr),
        in_specs=[pl.BlockSpec(memory_space=pltpu.SMEM),
                  pl.BlockSpec((tr, c), lambda h, i: (i, 0))],
        out_specs=pl.BlockSpec((None, tr, c), lambda h, i: (h, i, 0)),
        out_shape=jax.ShapeDtypeStruct((N_HEADS, r, c), f32),
        compiler_params=_cparams(("parallel", "parallel")), name="t5_bias",
    )(rel_bias, dist)


def _compress_accumulate(load_rows, pos_ref, wk_ref, wv_ref, rows):
    acc_k = jnp.zeros((rows, KV_W), f32)
    acc_v = jnp.zeros((rows, KV_W), f32)
    for c in range(CMP_BLOCK):
        lo = c * PAIR_W
        xk = (load_rows(c, 0) + pos_ref[:, lo:lo + KV_W]).astype(bf16)
        xv = (load_rows(c, 1) + pos_ref[:, lo + KV_W:lo + PAIR_W]).astype(bf16)
        acc_k = acc_k + jnp.dot(xk, wk_ref[c], preferred_element_type=f32)
        acc_v = acc_v + jnp.dot(xv, wv_ref[c], preferred_element_type=f32)
    return acc_k, acc_v


def _compress_prompt(kv_raw, pos_row, wk, wv, batch):
    rows_total = kv_raw.shape[1] // CMP_BLOCK
    n = rows_total // batch
    x = kv_raw.reshape(kv_raw.shape[0], rows_total, CMP_BLOCK * PAIR_W)

    def body(x_ref, pos_ref, wk_ref, wv_ref, kc_ref, vct_ref):
        def load_rows(c, kv):
            lo = c * PAIR_W + kv * KV_W
            return x_ref[:, lo:lo + KV_W]
        acc_k, acc_v = _compress_accumulate(load_rows, pos_ref, wk_ref, wv_ref, n)
        kc_ref[...] = acc_k.astype(bf16)
        vct_ref[...] = acc_v.T.astype(bf16)

    return pl.pallas_call(
        body, grid=(batch,),
        in_specs=[pl.BlockSpec((None, n, CMP_BLOCK * PAIR_W), lambda b: (0, b, 0)),
                  pl.BlockSpec(pos_row.shape, lambda b: (0, 0)),
                  pl.BlockSpec(wk.shape, lambda b: (0, 0, 0)),
                  pl.BlockSpec(wv.shape, lambda b: (0, 0, 0))],
        out_specs=[pl.BlockSpec((None, n, KV_W), lambda b: (b, 0, 0)),
                   pl.BlockSpec((None, KV_W, n), lambda b: (b, 0, 0))],
        out_shape=[jax.ShapeDtypeStruct((batch, n, KV_W), bf16),
                   jax.ShapeDtypeStruct((batch, KV_W, n), bf16)],
        compiler_params=_cparams(("parallel",)), name="compress_prompt",
    )(x, pos_row, wk, wv)


CMP_PAGES_PER_STEP = 32


CMP_ROW_PITCH = CMP_BLOCK + SUBLANES


def _compress_sample(cache_t, layer, page_table, pos_row, wk, wv):
    batch, n_pages = page_table.shape
    blocks_per_page = PAGE_SIZE // CMP_BLOCK
    pp = CMP_PAGES_PER_STEP
    steps = n_pages // pp
    rows = pp * blocks_per_page
    n_lane_tiles = PAIR_W // LANES
    page_pitch = blocks_per_page * CMP_ROW_PITCH

    def body(pt_ref, cache_ref, pos_ref, wk_ref, wv_ref, kc_ref, vct_ref, xbuf, sems, tscr):
        b, s = pl.program_id(0), pl.program_id(1)

        def page_copy(p):
            return pltpu.make_async_copy(cache_ref.at[layer, pt_ref[b, s * pp + p]], xbuf.at[p], sems.at[p])
        for p in range(pp):
            page_copy(p).start()
        for p in range(pp):
            page_copy(p).wait()

        def untranspose(p, carry):
            base = pl.multiple_of(p * page_pitch, SUBLANES)
            for j in range(n_lane_tiles):
                tile = xbuf[p, j * LANES:(j + 1) * LANES, :].T
                for n in range(blocks_per_page):
                    tscr[j, pl.ds(base + n * CMP_ROW_PITCH, CMP_BLOCK), :] = tile[n * CMP_BLOCK:(n + 1) * CMP_BLOCK]
            return carry
        lax.fori_loop(0, pp, untranspose, 0)

        def load_rows(c, kv):
            tiles = [tscr[kv * (KV_W // LANES) + j, pl.ds(c, rows, stride=CMP_ROW_PITCH), :]
                     for j in range(KV_W // LANES)]
            return jnp.concatenate(tiles, axis=1)
        acc_k, acc_v = _compress_accumulate(load_rows, pos_ref, wk_ref, wv_ref, rows)
        kc_ref[...] = acc_k.astype(bf16)
        vct_ref[...] = acc_v.T.astype(bf16)

    grid_spec = pltpu.PrefetchScalarGridSpec(
        num_scalar_prefetch=1, grid=(batch, steps),
        in_specs=[pl.BlockSpec(memory_space=pl.ANY),
                  pl.BlockSpec(pos_row.shape, lambda b, s, pt: (0, 0)),
                  pl.BlockSpec(wk.shape, lambda b, s, pt: (0, 0, 0)),
                  pl.BlockSpec(wv.shape, lambda b, s, pt: (0, 0, 0))],
        out_specs=[pl.BlockSpec((None, rows, KV_W), lambda b, s, pt: (b, s, 0)),
                   pl.BlockSpec((None, KV_W, rows), lambda b, s, pt: (b, 0, s))],
        scratch_shapes=[pltpu.VMEM((pp, PAIR_W, PAGE_SIZE), f32), pltpu.SemaphoreType.DMA((pp,)),
                        pltpu.VMEM((n_lane_tiles, pp * page_pitch, LANES), f32)])
    n_cmp = n_pages * blocks_per_page
    return pl.pallas_call(
        body, grid_spec=grid_spec,
        out_shape=[jax.ShapeDtypeStruct((batch, n_cmp, KV_W), bf16),
                   jax.ShapeDtypeStruct((batch, KV_W, n_cmp), bf16)],
        compiler_params=_cparams(("arbitrary", "arbitrary")), name="compress_sample",
    )(page_table, cache_t, pos_row, wk, wv)


def _softmax_cols(s, mask):
    sm = jnp.where(mask, s, NEG)
    m = jnp.max(sm, axis=0, keepdims=True)
    e = jnp.exp(sm - m)
    p = e / jnp.sum(e, axis=0, keepdims=True)
    return jnp.where(mask, p, 0.0)


def _topk_mask(score, score_scr, n_rows, k):
    rows = lax.broadcasted_iota(jnp.int32, score.shape, 0)
    score_scr[...] = score

    def step(j, cnt):
        sj = score_scr[pl.ds(j, 1), :]
        ahead = (sj > score) | ((sj == score) & (j < rows))
        return cnt + jnp.where(ahead, 1.0, 0.0)

    cnt = lax.fori_loop(0, n_rows, step, jnp.zeros(score.shape, f32))
    return jnp.where(cnt < k, 1.0, 0.0)


def _topk_mask_unrolled(score, score_scr, k):
    n_rows, c = score.shape
    score_scr[...] = score
    n_tiles = n_rows // SUBLANES
    tiles = [score[SUBLANES * v:SUBLANES * (v + 1)] for v in range(n_tiles)]
    row_in_tile = lax.broadcasted_iota(jnp.int32, (SUBLANES, c), 0)
    cnt = [jnp.zeros((SUBLANES, c), f32) for _ in range(n_tiles)]
    for j in range(n_rows):
        sj = score_scr[j:j + 1, :]
        vj = j // SUBLANES
        for v in range(n_tiles):
            if v > vj:
                ahead = sj >= tiles[v]
            elif v < vj:
                ahead = sj > tiles[v]
            else:
                ahead = (sj > tiles[v]) | ((sj == tiles[v]) & (row_in_tile > j % SUBLANES))
            cnt[v] = cnt[v] + jnp.where(ahead, 1.0, 0.0)
    return jnp.concatenate([jnp.where(cv < k, 1.0, 0.0) for cv in cnt], axis=0)


def _nsa_prompt(q, kc, vct, k_rows, vt_rows, gates, bias_c, bias_t, batch, seq):
    nq = seq // QB
    n_cmp = seq // CMP_BLOCK
    n_sel = seq // SEL_BLOCK
    ratio = SEL_BLOCK // CMP_BLOCK
    k_eff = min(TOP_K, n_sel)
    wq = HPG * QB
    win_chunks = WINDOW // KB

    def body(q_ref, kc_ref, vct_ref, ks_ref, vst_ref, kw_ref, vwt_ref, g_ref, bc_ref, bt_ref,
             o_ref, imp_scr, score_scr, sel_scr, gate_scr):
        g = pl.program_id(1)
        i = pl.program_id(2)
        q0 = i * QB

        qt = q_ref[...].astype(f32).T
        qcat = jnp.concatenate([qt[h * HEAD_DIM:(h + 1) * HEAD_DIM] for h in range(HPG)], axis=1)
        qrep = jnp.concatenate([qcat] * N_KV, axis=0)
        row_grp = lax.broadcasted_iota(jnp.int32, (KV_W, wq), 0) // HEAD_DIM
        qpad = jnp.where(row_grp == g, qrep, 0.0).astype(bf16)

        t_row = q0 + lax.broadcasted_iota(jnp.int32, (1, wq), 1) % QB

        s_c = jnp.dot(kc_ref[...], qpad, preferred_element_type=f32) + bc_ref[...]
        c_end = lax.broadcasted_iota(jnp.int32, (n_cmp, wq), 0) * CMP_BLOCK + (CMP_BLOCK - 1)
        p_c = _softmax_cols(s_c, c_end <= t_row)
        o_c = jnp.dot(vct_ref[...], p_c.astype(bf16), preferred_element_type=f32)

        imp_scr[...] = sum(p_c[:, h * QB:(h + 1) * QB] for h in range(HPG))
        imp = sum(imp_scr[pl.ds(r, n_sel, stride=ratio), :] for r in range(ratio))
        blk = lax.broadcasted_iota(jnp.int32, (n_sel, QB), 0)
        t_q = q0 + lax.broadcasted_iota(jnp.int32, (n_sel, QB), 1)
        cur = t_q // SEL_BLOCK
        forced = (blk == 0) | (blk == cur) | (blk == cur - 1)
        score = imp + jnp.where(forced, FORCE_BONUS, 0.0)
        score = jnp.where(blk * SEL_BLOCK <= t_q, score, NEG)
        sel_scr[...] = _topk_mask_unrolled(score, score_scr, k_eff)

        kj = lax.broadcasted_iota(jnp.int32, (KB, QB), 0)
        qi = lax.broadcasted_iota(jnp.int32, (KB, QB), 1)
        rel = qi - kj
        upper = kj < SEL_BLOCK

        def scores(k_ref, chunk, bias, mask):
            k0 = pl.multiple_of(chunk * KB, KB)
            s = jnp.dot(k_ref[pl.ds(k0, KB), :], qpad, preferred_element_type=f32)
            penalty = jnp.where(mask, 0.0, NEG)
            return s + bias + jnp.concatenate([penalty] * HPG, axis=1)

        def weighted_values(vt_ref, chunks, ps):
            return sum(jnp.dot(vt_ref[:, pl.ds(pl.multiple_of(c * KB, KB), KB)], p.astype(bf16),
                               preferred_element_type=f32) for c, p in zip(chunks, ps))

        def col_max(ss):
            return functools.reduce(jnp.maximum, [jnp.max(s, axis=0, keepdims=True) for s in ss])

        def sel_step(j, carry):
            m, l, acc = carry
            chunks, ss = [], []
            for r in range(SEL_SUB):
                c = j * SEL_SUB + r
                cc = jnp.minimum(c, i)
                rows2 = sel_scr[pl.ds(pl.multiple_of(cc * (KB // SEL_BLOCK), KB // SEL_BLOCK),
                                      KB // SEL_BLOCK), :]
                chosen = jnp.where(upper, rows2[0:1, :], rows2[1:2, :]) > 0.5
                mask = chosen & (rel + (i - c) * KB >= 0)
                ss.append(scores(ks_ref, cc, bt_ref[jnp.minimum(i - cc, N_BIAS_TILES - 1)], mask))
                chunks.append(cc)
            m_new = jnp.maximum(m, col_max(ss))
            alpha = jnp.exp(m - m_new)
            ps = [jnp.exp(s - m_new) for s in ss]
            l = alpha * l + sum(jnp.sum(p, axis=0, keepdims=True) for p in ps)
            acc = alpha * acc + weighted_values(vst_ref, chunks, ps)
            return m_new, l, acc

        init = (jnp.full((1, wq), NEG, f32), jnp.zeros((1, wq), f32), jnp.zeros((HEAD_DIM, wq), f32))
        _, l_s, acc_s = lax.fori_loop(0, (i + SEL_SUB) // SEL_SUB, sel_step, init)
        o_s = acc_s / l_s

        chunks, ss = [], []
        for dd in range(win_chunks + 1):
            c = i - dd
            dist = rel + (dd * KB + jnp.where(c >= 0, 0, 2 * WINDOW))
            mask = (dist >= 0) & (dist < WINDOW)
            chunks.append(jnp.maximum(c, 0))
            ss.append(scores(kw_ref, chunks[-1], bt_ref[dd], mask))
        m_w = col_max(ss)
        ps = [jnp.exp(s - m_w) for s in ss]
        l_w = sum(jnp.sum(p, axis=0, keepdims=True) for p in ps)
        o_w = weighted_values(vwt_ref, chunks, ps) / l_w

        gate_scr[...] = g_ref[...].T
        gg = gate_scr[pl.ds(pl.multiple_of(g * NG_SLOT, NG_SLOT), NG_SLOT), :]
        outs = []
        for h in range(HPG):
            sl = slice(h * QB, (h + 1) * QB)
            outs.append(gg[3 * h:3 * h + 1] * o_c[:, sl] + gg[3 * h + 1:3 * h + 2] * o_s[:, sl]
                        + gg[3 * h + 2:3 * h + 3] * o_w[:, sl])
        o_ref[...] = jnp.concatenate(outs, axis=0).T.astype(o_ref.dtype)

    return pl.pallas_call(
        body, grid=(batch, N_KV, nq),
        in_specs=[
            pl.BlockSpec((QB, KV_W), lambda b, g, i: (b * nq + i, g)),
            pl.BlockSpec((None, n_cmp, KV_W), lambda b, g, i: (b, 0, 0)),
            pl.BlockSpec((None, HEAD_DIM, n_cmp), lambda b, g, i: (b, g, 0)),
            pl.BlockSpec((None, seq, KV_W), lambda b, g, i: (1, b, 0)),
            pl.BlockSpec((None, HEAD_DIM, seq), lambda b, g, i: (1, g, b)),
            pl.BlockSpec((None, seq, KV_W), lambda b, g, i: (2, b, 0)),
            pl.BlockSpec((None, HEAD_DIM, seq), lambda b, g, i: (2, g, b)),
            pl.BlockSpec((QB, NG_W), lambda b, g, i: (b * nq + i, 0)),
            pl.BlockSpec((None, None, n_cmp, wq), lambda b, g, i: (i, g, 0, 0)),
            pl.BlockSpec((N_BIAS_TILES, None, KB, wq), lambda b, g, i: (0, g, 0, 0)),
        ],
        out_specs=pl.BlockSpec((QB, KV_W), lambda b, g, i: (b * nq + i, g)),
        out_shape=jax.ShapeDtypeStruct((batch * seq, N_HEADS * HEAD_DIM), bf16),
        scratch_shapes=[pltpu.VMEM((n_cmp, QB), f32), pltpu.VMEM((n_sel, QB), f32),
                        pltpu.VMEM((n_sel, QB), f32), pltpu.VMEM((NG_W, QB), f32)],
        compiler_params=_cparams(("parallel", "parallel", "arbitrary")), name="nsa_prompt",
    )(q, kc, vct, k_rows, vt_rows, k_rows, vt_rows, gates, bias_c, bias_t)


def _retention(qk, v, rg, gn_w, gn_b, state0, dec_in, dec_q, dec_k, dec_c, batch, length, chunk, out_dtype):
    n = length // chunk

    def body(q_ref, k_ref, v_ref, rg_ref, gw_ref, gb_ref, s0_ref, din_ref, dq_ref, dk_ref, dc_ref,
             r_ref, s_ref):
        @pl.when(pl.program_id(2) == 0)
        def _():
            s_ref[...] = s0_ref[...]

        qc = q_ref[...].astype(bf16)
        kf = k_ref[...].astype(f32)
        vc = v_ref[...].astype(bf16)
        st = s_ref[...]
        inner = lax.dot_general(qc, kf.astype(bf16), (((1,), (1,)), ((), ())),
                                preferred_element_type=f32) * din_ref[...]
        o = jnp.dot(inner.astype(bf16), vc, preferred_element_type=f32)
        o = o + jnp.dot(qc, st.astype(bf16), preferred_element_type=f32) * dq_ref[...]
        kd = (kf * dk_ref[...]).astype(bf16)
        s_ref[...] = st * dc_ref[...] + lax.dot_general(kd, vc, (((0,), (0,)), ((), ())),
                                                        preferred_element_type=f32)
        mu = jnp.mean(o, axis=-1, keepdims=True)
        var = jnp.mean(jnp.square(o - mu), axis=-1, keepdims=True)
        y = (o - mu) * lax.rsqrt(var + NORM_EPS) * gw_ref[...] + gb_ref[...]
        gate = rg_ref[...]
        r_ref[...] = (gate * jax.nn.sigmoid(gate) * y).astype(r_ref.dtype)

    seq_spec = lambda w, col0=0: pl.BlockSpec((chunk, w), lambda b, h, c: (b * n + c, h + col0))
    head_spec = lambda shape: pl.BlockSpec((None,) + shape, lambda b, h, c: (h,) + (0,) * len(shape))
    return pl.pallas_call(
        body, grid=(batch, RET_HEADS, n),
        in_specs=[seq_spec(RET_DK), seq_spec(RET_DK, RET_HEADS), seq_spec(RET_DV), seq_spec(RET_DV),
                  pl.BlockSpec((1, RET_DV), lambda b, h, c: (0, h)),
                  pl.BlockSpec((1, RET_DV), lambda b, h, c: (0, h)),
                  pl.BlockSpec((None, None, RET_DK, RET_DV), lambda b, h, c: (b, h, 0, 0)),
                  head_spec((chunk, chunk)), head_spec((chunk, 1)), head_spec((chunk, 1)),
                  head_spec((1, 1))],
        out_specs=[seq_spec(RET_DV),
                   pl.BlockSpec((None, None, RET_DK, RET_DV), lambda b, h, c: (b, h, 0, 0))],
        out_shape=[jax.ShapeDtypeStruct((batch * length, RET_HEADS * RET_DV), out_dtype),
                   jax.ShapeDtypeStruct((batch, RET_HEADS, RET_DK, RET_DV), f32)],
        compiler_params=_cparams(("parallel", "parallel", "arbitrary")), name="retention",
    )(qk, qk, v, rg, gn_w, gn_b, state0, dec_in, dec_q, dec_k, dec_c)


def _retention_decays(chunk):
    lg = jnp.log(1.0 - jnp.exp2(-5.0 - jnp.arange(RET_HEADS, dtype=f32)))
    i = jnp.arange(chunk, dtype=f32)
    diff = i[:, None] - i[None, :]
    dec_in = jnp.where(diff >= 0, jnp.exp(jnp.maximum(diff, 0.0)[None] * lg[:, None, None]), 0.0)
    dec_q = jnp.exp((i[None, :] + 1.0) * lg[:, None])[..., None]
    dec_k = jnp.exp((chunk - 1.0 - i)[None, :] * lg[:, None])[..., None]
    dec_c = jnp.exp(chunk * lg)[:, None, None]
    return dec_in, dec_q, dec_k, dec_c


def _rope_tables(pos):
    half = RET_DK // 2
    inv = ROPE_BASE ** (-jnp.arange(half, dtype=f32) / half)
    ang = pos.astype(f32)[:, None] * inv[None, :]
    return jnp.cos(ang), jnp.sin(ang)


def _nsa_sample_cmp(kc, vct, qpt, bias_c, t_pos_row, n_sel, n_sel_pad):
    batch, n_cmp, _ = kc.shape
    ratio = SEL_BLOCK // CMP_BLOCK
    cols = qpt.shape[2]
    grp_cols = cols // HPG

    def body(kc_ref, vct_ref, q_ref, bc_ref, t_ref, oc_ref, sel_ref, imp_scr, score_scr):
        t_row = t_ref[...]
        s_c = jnp.dot(kc_ref[...], q_ref[...], preferred_element_type=f32) + bc_ref[...]
        c_end = lax.broadcasted_iota(jnp.int32, (n_cmp, cols), 0) * CMP_BLOCK + (CMP_BLOCK - 1)
        p_c = _softmax_cols(s_c, c_end <= t_row)
        oc_ref[...] = jnp.dot(vct_ref[...], p_c.astype(bf16), preferred_element_type=f32)
        imp_rep = p_c
        for h in range(1, HPG):
            imp_rep = imp_rep + pltpu.roll(p_c, h * grp_cols, axis=1)
        imp_scr[...] = jnp.zeros(imp_scr.shape, f32)
        imp_scr[pl.ds(0, n_cmp), :] = imp_rep
        imp = sum(imp_scr[pl.ds(r, n_sel_pad, stride=ratio), :] for r in range(ratio))
        blk = lax.broadcasted_iota(jnp.int32, (n_sel_pad, cols), 0)
        cur = t_row // SEL_BLOCK
        forced = (blk == 0) | (blk == cur) | (blk == cur - 1)
        score = imp + jnp.where(forced, FORCE_BONUS, 0.0)
        score = jnp.where(blk * SEL_BLOCK <= t_row, score, NEG)
        score = jnp.where(blk < n_sel, score, -jnp.inf)
        sel_ref[...] = _topk_mask(score, score_scr, n_sel, min(TOP_K, n_sel))

    return pl.pallas_call(
        body, grid=(batch,),
        in_specs=[pl.BlockSpec((None, n_cmp, KV_W), lambda b: (b, 0, 0)),
                  pl.BlockSpec((None, KV_W, n_cmp), lambda b: (b, 0, 0)),
                  pl.BlockSpec((None, KV_W, cols), lambda b: (b, 0, 0)),
                  pl.BlockSpec((n_cmp, cols), lambda b: (0, 0)),
                  pl.BlockSpec((1, cols), lambda b: (0, 0))],
        out_specs=[pl.BlockSpec((None, KV_W, cols), lambda b: (b, 0, 0)),
                   pl.BlockSpec((None, n_sel_pad, cols), lambda b: (b, 0, 0))],
        out_shape=[jax.ShapeDtypeStruct((batch, KV_W, cols), f32),
                   jax.ShapeDtypeStruct((batch, n_sel_pad, cols), f32)],
        scratch_shapes=[pltpu.VMEM((n_sel_pad * ratio, cols), f32), pltpu.VMEM((n_sel_pad, cols), f32)],
        compiler_params=_cparams(("parallel",)), name="nsa_sample_cmp",
    )(kc, vct, qpt, bias_c, t_pos_row)


SEL_PAGES_PER_STEP = 16


def _softmax_rows(s, mask):
    sm = jnp.where(mask, s, NEG)
    m = jnp.max(sm, axis=1, keepdims=True)
    e = jnp.exp(sm - m)
    p = e / jnp.sum(e, axis=1, keepdims=True)
    return jnp.where(mask, p, 0.0)


def _nsa_sample_sel(cache_t, layer, page_table, sel_new, win_all, qp, sel_mask, oc_t, gates_c,
                    bias_s, bias_new, bias_w, t_pos_col):
    batch, n_pages = page_table.shape
    pp = SEL_PAGES_PER_STEP
    steps = n_pages // pp
    keys = pp * PAGE_SIZE
    nq = qp.shape[1]
    blocks_per_page = PAGE_SIZE // SEL_BLOCK
    nb = pp * blocks_per_page
    n_win_pad = win_all.shape[1]
    w_buf = min(WINDOW, PAST_LEN)
    nt = (((1,), (1,)), ((), ()))
    tn = (((0,), (0,)), ((), ()))

    def body(pt_ref, cache_ref, new_ref, win_ref, q_ref, selm_ref, oc_ref, g_ref, bs_ref, bn_ref, bw_ref,
             t_ref, o_ref, xbuf, sems, m_scr, l_scr, acc_scr):
        b, s = pl.program_id(0), pl.program_id(1)

        def page_copy(p):
            return pltpu.make_async_copy(cache_ref.at[layer, pt_ref[b, s * pp + p]], xbuf.at[p], sems.at[p])
        for p in range(pp):
            page_copy(p).start()

        @pl.when(s == 0)
        def _():
            m_scr[...] = jnp.full(m_scr.shape, NEG, f32)
            l_scr[...] = jnp.zeros(l_scr.shape, f32)
            acc_scr[...] = jnp.zeros(acc_scr.shape, f32)

        q = q_ref[...]
        t_col = t_ref[...]

        def chosen_keys(block_rows, n_keys):
            n_blocks = block_rows.shape[0]
            key_blk = lax.broadcasted_iota(jnp.int32, (n_blocks, n_keys), 1) // SEL_BLOCK
            spread = jnp.where(key_blk == lax.broadcasted_iota(jnp.int32, (n_blocks, n_keys), 0), 1.0, 0.0)
            return lax.dot_general(block_rows, spread, tn, preferred_element_type=f32) > 0.5

        def online(s_qk, mask, weighted_values):
            s_m = jnp.where(mask, s_qk, NEG)
            m_old = m_scr[...]
            m_new = jnp.maximum(m_old, jnp.max(s_m, axis=1, keepdims=True))
            alpha = jnp.exp(m_old - m_new)
            p = jnp.where(mask, jnp.exp(s_m - m_new), 0.0)
            l_scr[...] = alpha * l_scr[...] + jnp.sum(p, axis=1, keepdims=True)
            acc_scr[...] = alpha * acc_scr[...] + weighted_values(p.astype(bf16))
            m_scr[...] = m_new

        for p in range(pp):
            page_copy(p).wait()
        s_qk = jnp.concatenate(
            [jnp.dot(q, xbuf[p, :KV_W, :].astype(bf16), preferred_element_type=f32) for p in range(pp)],
            axis=1) + bs_ref[...]
        mask = chosen_keys(selm_ref[pl.ds(pl.multiple_of(s * nb, nb), nb), :], keys)
        online(s_qk, mask, lambda pb: sum(
            lax.dot_general(pb[:, p * PAGE_SIZE:(p + 1) * PAGE_SIZE], xbuf[p, KV_W:, :].astype(bf16), nt,
                            preferred_element_type=f32) for p in range(pp)))

        @pl.when(s == steps - 1)
        def _():
            kv_n = new_ref[...]
            s_n = lax.dot_general(q, kv_n[:, :KV_W].astype(bf16), nt, preferred_element_type=f32) + bn_ref[...]
            pos_n = PAST_LEN + lax.broadcasted_iota(jnp.int32, (nq, PAGE_SIZE), 1)
            mask_n = chosen_keys(selm_ref[pl.ds(n_pages * blocks_per_page, SUBLANES), :], PAGE_SIZE) & (pos_n <= t_col)
            online(s_n, mask_n, lambda pb: jnp.dot(pb, kv_n[:, KV_W:].astype(bf16), preferred_element_type=f32))
            o_s = acc_scr[...] / l_scr[...]

            kv_w = win_ref[...]
            kpos = (PAST_LEN - w_buf) + lax.broadcasted_iota(jnp.int32, (nq, n_win_pad), 1)
            dist = t_col - kpos
            mask_w = (dist >= 0) & (dist < WINDOW) & (kpos >= 0)
            s_w = lax.dot_general(q, kv_w[:, :KV_W].astype(bf16), nt, preferred_element_type=f32) + bw_ref[...]
            p_w = _softmax_rows(s_w, mask_w)
            o_w = jnp.dot(p_w.astype(bf16), kv_w[:, KV_W:].astype(bf16), preferred_element_type=f32)
            gts = g_ref[...]
            o_ref[...] = gts[:, 0:1] * oc_ref[...].T + gts[:, 1:2] * o_s + gts[:, 2:3] * o_w

    grid_spec = pltpu.PrefetchScalarGridSpec(
        num_scalar_prefetch=1, grid=(batch, steps),
        in_specs=[pl.BlockSpec(memory_space=pl.ANY),
                  pl.BlockSpec((None, PAGE_SIZE, PAIR_W), lambda b, s, pt: (b, 0, 0)),
                  pl.BlockSpec((None, n_win_pad, PAIR_W), lambda b, s, pt: (b, 0, 0)),
                  pl.BlockSpec((None, nq, KV_W), lambda b, s, pt: (b, 0, 0)),
                  pl.BlockSpec((None, sel_mask.shape[1], nq), lambda b, s, pt: (b, 0, 0)),
                  pl.BlockSpec((None, KV_W, nq), lambda b, s, pt: (b, 0, 0)),
                  pl.BlockSpec((None, nq, LANES), lambda b, s, pt: (b, 0, 0)),
                  pl.BlockSpec((nq, keys), lambda b, s, pt: (0, s)),
                  pl.BlockSpec((nq, PAGE_SIZE), lambda b, s, pt: (0, 0)),
                  pl.BlockSpec((nq, n_win_pad), lambda b, s, pt: (0, 0)),
                  pl.BlockSpec((nq, 1), lambda b, s, pt: (0, 0))],
        out_specs=pl.BlockSpec((None, nq, KV_W), lambda b, s, pt: (b, 0, 0)),
        scratch_shapes=[pltpu.VMEM((pp, PAIR_W, PAGE_SIZE), f32), pltpu.SemaphoreType.DMA((pp,)),
                        pltpu.VMEM((nq, 1), f32), pltpu.VMEM((nq, 1), f32),
                        pltpu.VMEM((nq, KV_W), f32)])
    return pl.pallas_call(
        body, grid_spec=grid_spec,
        out_shape=jax.ShapeDtypeStruct((batch, nq, KV_W), f32),
        compiler_params=_cparams(("arbitrary", "arbitrary")), name="nsa_sample_sel",
    )(page_table, cache_t, sel_new, win_all, qp, sel_mask, oc_t, gates_c, bias_s, bias_new, bias_w,
      t_pos_col)


_OFF_Q = 0
_OFF_KV = N_HEADS * HEAD_DIM
_OFF_NG = _OFF_KV + 6 * KV_W
_OFF_RQ = _OFF_NG + 3 * N_HEADS
_OFF_RV = _OFF_RQ + 2 * RET_HEADS * RET_DK
_OFF_RG = _OFF_RV + RET_HEADS * RET_DV
_D_IN = _OFF_RG + RET_HEADS * RET_DV + 2 * D_MODEL


def _prep_w_in(w_in_l):
    w = w_in_l.astype(bf16)
    w_attn = w[:, :_OFF_NG]
    ng = w[:, _OFF_NG:_OFF_RQ].reshape(D_MODEL, N_KV, 3 * HPG)
    w_ng = jnp.pad(ng, ((0, 0), (0, 0), (0, NG_SLOT - 3 * HPG))).reshape(D_MODEL, N_KV * NG_SLOT)
    w_ng = jnp.pad(w_ng, ((0, 0), (0, NG_W - N_KV * NG_SLOT)))
    w_ret = w[:, _OFF_RQ:]
    return w_attn, w_ng, w_ret


def _prep_w_phi(w_phi_l, cmp_pos_l):
    eye = jnp.eye(N_KV, dtype=f32)
    wk = jnp.einsum('gh,cde->cgdhe', eye, w_phi_l[0]).reshape(CMP_BLOCK, KV_W, KV_W).astype(bf16)
    wv = jnp.einsum('gh,cde->cgdhe', eye, w_phi_l[1]).reshape(CMP_BLOCK, KV_W, KV_W).astype(bf16)
    pos = jnp.broadcast_to(cmp_pos_l[:, :, None, :], (2, CMP_BLOCK, N_KV, HEAD_DIM))
    pos_row = jnp.transpose(pos, (1, 0, 2, 3)).reshape(1, CMP_BLOCK * PAIR_W)
    return wk, wv, pos_row


def _layer(x, mod, lw, rows_per_batch, tm, act, kv_layouts, rope, attend):
    m = x.shape[0]
    sh1, sc1, g1, sh2, sc2, g2 = mod
    h = _norm("norm_mix", x, lw["norm_mix"], sc1, sh1, rows_per_batch, act)

    def plain(dtype, scale=1.0):
        def epi(accs, e_refs, o_refs):
            o_refs[0][...] = (accs[0] * scale).astype(dtype)
        return epi

    w_attn, w_ng, w_ret = lw["w_attn"], lw["w_ng"], lw["w_ret"]
    tn = 512
    q = _linear("proj_q", [h], [w_attn[:, :_OFF_KV]], [], [], plain(act, HEAD_DIM ** -0.5),
                jax.ShapeDtypeStruct((m, _OFF_KV), act), _tile_spec(tm, tn), tm, tn)
    assert tn == PAIR_W
    pair_spec = lambda w: pl.BlockSpec((None, tm, w), lambda i, j: (j, i, 0))
    if kv_layouts:
        def kv_epi(accs, e_refs, o_refs):
            a = accs[0]
            o_refs[0][...] = a
            o_refs[1][...] = a[:, :KV_W].astype(bf16)
            o_refs[2][...] = a[:, KV_W:].T.astype(bf16)
        kv_raw = _linear("proj_kv", [h], [w_attn[:, _OFF_KV:]], [], [], kv_epi,
                         [jax.ShapeDtypeStruct((3, m, PAIR_W), f32), jax.ShapeDtypeStruct((3, m, KV_W), bf16),
                          jax.ShapeDtypeStruct((3, KV_W, m), bf16)],
                         [pair_spec(PAIR_W), pair_spec(KV_W),
                          pl.BlockSpec((None, KV_W, tm), lambda i, j: (j, 0, i))], tm, tn)
    else:
        kv_raw = _linear("proj_kv", [h], [w_attn[:, _OFF_KV:]], [], [], plain(f32),
                         jax.ShapeDtypeStruct((3, m, PAIR_W), f32), pair_spec(PAIR_W), tm, tn)

    def sig_epi(accs, e_refs, o_refs):
        o_refs[0][...] = jax.nn.sigmoid(accs[0])
    gates = _linear("proj_ng", [h], [w_ng], [], [], sig_epi,
                    jax.ShapeDtypeStruct((m, NG_W), f32), _tile_spec(tm, NG_W), tm, NG_W)

    cos, sin, rope_spec = rope
    n_qk_tiles = RET_HEADS * RET_DK // tn

    def rot_epi(accs, e_refs, o_refs):
        a = accs[0]
        c, s = e_refs[0][...], e_refs[1][...]
        scale = jnp.where(pl.program_id(1) >= n_qk_tiles, RET_DK ** -0.5, 1.0)
        half = RET_DK // 2
        outs = []
        for hh in range(tn // RET_DK):
            x1 = a[:, hh * RET_DK:hh * RET_DK + half]
            x2 = a[:, hh * RET_DK + half:(hh + 1) * RET_DK]
            outs += [x1 * c - x2 * s, x2 * c + x1 * s]
        o_refs[0][...] = (jnp.concatenate(outs, axis=1) * scale).astype(act)

    n_rot = 2 * RET_HEADS * RET_DK
    rqk = _linear("proj_rqk", [h], [w_ret[:, :n_rot]], [cos, sin], [rope_spec, rope_spec], rot_epi,
                  jax.ShapeDtypeStruct((m, n_rot), act), _tile_spec(tm, tn), tm, tn)
    n_rv = RET_HEADS * RET_DV
    rv = _linear("proj_rv", [h], [w_ret[:, n_rot:n_rot + n_rv]], [], [], plain(act),
                 jax.ShapeDtypeStruct((m, n_rv), act), _tile_spec(tm, tn), tm, tn)
    gts = _linear("proj_gates", [h], [w_ret[:, n_rot + n_rv:]], [], [], plain(f32),
                  jax.ShapeDtypeStruct((m, 3 * D_MODEL), f32), _tile_spec(tm, tn), tm, tn)

    o_nsa, r, states = attend(q, kv_raw, gates, rqk, rv, gts)

    def merge_epi(accs, e_refs, o_refs):
        o_refs[0][...] = (jax.nn.sigmoid(e_refs[0][...]) * accs[0]
                          + jax.nn.sigmoid(e_refs[1][...]) * accs[1]).astype(act)
    nt = D_MODEL // tn
    merged = _linear("merge", [o_nsa, r], [lw["w_nsa_proj"], lw["w_ret_proj"]], [gts, gts],
                     [_tile_spec(tm, tn, nt), _tile_spec(tm, tn, 2 * nt)], merge_epi,
                     jax.ShapeDtypeStruct((m, D_MODEL), act), _tile_spec(tm, tn), tm, tn)

    def resid_epi(accs, e_refs, o_refs):
        o_refs[0][...] = e_refs[0][...] + e_refs[1][...] * accs[0]
    x = _linear("out_proj", [merged], [lw["w_out"]], [x, g1],
                [_tile_spec(tm, tn), _mod_spec(g1, tm, tn, rows_per_batch)], resid_epi,
                jax.ShapeDtypeStruct((m, D_MODEL), f32), _tile_spec(tm, tn), tm, tn)

    h2 = _norm("norm_mlp", x, lw["norm_mlp"], sc2, sh2, rows_per_batch, act)

    def up_epi(accs, e_refs, o_refs):
        o_refs[0][...] = jnp.square(jnp.maximum(accs[0], 0.0)).astype(act)
    u = _linear("mlp_up", [h2], [lw["w_up"]], [], [], up_epi,
                jax.ShapeDtypeStruct((m, D_FF), act), _tile_spec(tm, tn), tm, tn)
    tm_d = min(tm, 512)
    x = _linear("mlp_down", [u], [lw["w_down"]], [x, g2],
                [_tile_spec(tm_d, tn), _mod_spec(g2, tm_d, tn, rows_per_batch)], resid_epi,
                jax.ShapeDtypeStruct((m, D_MODEL), f32), _tile_spec(tm_d, tn), tm_d, tn)
    return x, states


def kernel(x_prompt, x_sample, c_prompt, c_sample, cache_cmp_kv, cache_sel_kv, state_win_kv, state_ret,
           page_table, rel_bias, w_mod, b_mod, norm_mix, norm_mlp, w_in, cmp_pos, w_phi, w_nsa_proj,
           ret_gn_w, ret_gn_b, w_ret_proj, w_out, w_up, w_down, norm_final):
    batch, seq, _ = x_prompt.shape
    dbatch, dseq, _ = x_sample.shape
    depth = w_in.shape[0]
    n_pool = cache_cmp_kv.shape[1]
    n_pages = page_table.shape[1]
    i32 = jnp.int32

    n_c = batch + dbatch
    c_all = jnp.pad(jnp.concatenate([c_prompt, c_sample], axis=0), ((0, (-n_c) % SUBLANES), (0, 0)))
    mod_all = _mod_all(c_all, w_mod, b_mod)

    nq = seq // QB
    n_cmp_p = seq // CMP_BLOCK
    kj = jnp.arange(KB, dtype=i32)[:, None]
    qi = jnp.arange(QB, dtype=i32)[None, :]
    dist_tiles = (jnp.arange(N_BIAS_TILES, dtype=i32)[:, None, None] * KB + qi[None] - kj[None])
    bias_t = _bias_from_dist(dist_tiles.reshape(N_BIAS_TILES * KB, QB), rel_bias)
    bias_t = bias_t.reshape(N_KV, HPG, N_BIAS_TILES, KB, QB).transpose(2, 0, 3, 1, 4).reshape(
        N_BIAS_TILES, N_KV, KB, HPG * QB)
    t_all = jnp.arange(seq, dtype=i32)
    c_end_p = jnp.arange(n_cmp_p, dtype=i32) * CMP_BLOCK + (CMP_BLOCK - 1)
    bias_c = _bias_from_dist(t_all[None, :] - c_end_p[:, None], rel_bias)
    bias_c = bias_c.reshape(N_KV, HPG, n_cmp_p, nq, QB).transpose(3, 0, 2, 1, 4).reshape(
        nq, N_KV, n_cmp_p, HPG * QB)

    cos_p, sin_p = _rope_tables(jnp.arange(seq, dtype=i32))
    dec_p = _retention_decays(RET_CHUNK)

    cols = HPG * N_KV * dseq
    t_cols = PAST_LEN + jnp.tile(jnp.arange(dseq, dtype=i32), HPG * N_KV)
    head_cols = jnp.tile(jnp.repeat(jnp.arange(N_KV, dtype=i32), dseq), HPG) * HPG + jnp.repeat(
        jnp.arange(HPG, dtype=i32), N_KV * dseq)
    t_new = PAST_LEN + jnp.arange(dseq, dtype=i32)

    def col_bias(key_pos):
        per_head = _bias_from_dist(t_new[:, None] - key_pos[None, :], rel_bias)
        t_idx = jnp.tile(jnp.arange(dseq, dtype=i32), HPG * N_KV)
        return per_head[head_cols, t_idx, :]

    n_cmp_s = PAST_LEN // CMP_BLOCK
    bias_c_s = col_bias(jnp.arange(n_cmp_s, dtype=i32) * CMP_BLOCK + (CMP_BLOCK - 1)).T
    bias_s_s = col_bias(jnp.arange(PAST_LEN, dtype=i32))
    bias_new_s = col_bias(PAST_LEN + jnp.arange(PAGE_SIZE, dtype=i32))
    w_buf = state_win_kv.shape[2]
    n_win_pad = -(-(w_buf + dseq) // LANES) * LANES
    bias_w_s = col_bias(PAST_LEN - w_buf + jnp.arange(n_win_pad, dtype=i32))
    n_sel_s = -(-(PAST_LEN + dseq) // SEL_BLOCK)
    n_sel_pad = -(-(n_pages * (PAGE_SIZE // SEL_BLOCK) + PAGE_SIZE // SEL_BLOCK) // SUBLANES) * SUBLANES
    assert n_sel_pad >= n_sel_s
    cos_s, sin_s = _rope_tables(t_new)
    cos_s, sin_s = jnp.tile(cos_s, (dbatch, 1)), jnp.tile(sin_s, (dbatch, 1))
    dec_s = _retention_decays(dseq)
    t_pos_row = t_cols.reshape(1, cols)
    t_pos_col = t_cols.reshape(cols, 1)

    page_major = lambda c: jnp.transpose(c, (0, 1, 3, 4, 5, 2)).reshape(depth, n_pool, PAIR_W, PAGE_SIZE)
    cache_cmp_t = page_major(cache_cmp_kv)
    cache_sel_t = page_major(cache_sel_kv)

    m_p = batch * seq
    m_s = dbatch * dseq
    tm_p = 1024
    tm_s = m_s

    xp = x_prompt.reshape(m_p, D_MODEL)
    xs = x_sample.reshape(m_s, D_MODEL)
    st_p, st_s = [], []
    for l in range(depth):
        w_attn, w_ng, w_ret = _prep_w_in(w_in[l])
        wk, wv, pos_row = _prep_w_phi(w_phi[l], cmp_pos[l])
        lw = dict(norm_mix=norm_mix[l], norm_mlp=norm_mlp[l], w_attn=w_attn, w_ng=w_ng, w_ret=w_ret,
                  w_nsa_proj=w_nsa_proj[l].astype(bf16), w_ret_proj=w_ret_proj[l].astype(bf16),
                  w_out=w_out[l].astype(bf16), w_up=w_up[l].astype(bf16), w_down=w_down[l].astype(bf16))
        gn_w = ret_gn_w[l].reshape(1, -1)
        gn_b = ret_gn_b[l].reshape(1, -1)

        mod_p = [a.reshape(batch, 1, D_MODEL) for a in jnp.split(mod_all[l, :batch], 6, axis=-1)]

        def attend_prompt(q, kv, gates, rqk, rv, gts):
            kv_raw, k_rows, vt_rows = kv
            kc, vct = _compress_prompt(kv_raw, pos_row, wk, wv, batch)
            o = _nsa_prompt(q, kc, vct, k_rows, vt_rows, gates, bias_c, bias_t, batch, seq)
            zero_state = jnp.zeros((batch, RET_HEADS, RET_DK, RET_DV), f32)
            r, s_new = _retention(rqk, rv, gts, gn_w, gn_b, zero_state, *dec_p, batch, seq, RET_CHUNK, bf16)
            kv6 = lambda a: a.reshape(batch, seq, 2, N_KV, HEAD_DIM)
            win = kv6(kv_raw[2])[:, seq - min(WINDOW, seq):]
            return o, r, (kv6(kv_raw[0]), kv6(kv_raw[1]), win, s_new)

        rope_p = (cos_p, sin_p, pl.BlockSpec((tm_p, RET_DK // 2), lambda i, j: (i % (seq // tm_p), 0)))
        xp, st = _layer(xp, mod_p, lw, seq, tm_p, bf16, True, rope_p, attend_prompt)
        st_p.append(st)

        mod_rows = jnp.repeat(mod_all[l, batch:batch + dbatch], dseq, axis=0)
        mod_s = jnp.split(mod_rows, 6, axis=-1)

        def attend_sample(q, kv_raw, gates, rqk, rv, gts):
            cmp_new, sel_new, win_new = (kv_raw[j].reshape(dbatch, dseq, PAIR_W) for j in range(3))
            kc, vct = _compress_sample(cache_cmp_t, l, page_table, pos_row, wk, wv)
            q5 = q.reshape(dbatch, dseq, N_KV, HPG, HEAD_DIM)
            qp = jnp.einsum('btghd,gk->bhgtkd', q5, jnp.eye(N_KV, dtype=q.dtype)).reshape(
                dbatch, cols, KV_W).astype(bf16)
            oc_t, sel_mask = _nsa_sample_cmp(kc, vct, jnp.swapaxes(qp, 1, 2), bias_c_s, t_pos_row,
                                             n_sel_s, n_sel_pad)
            g4 = gates[:, :N_KV * NG_SLOT].reshape(dbatch, dseq, N_KV, NG_SLOT)[..., :3 * HPG]
            g4 = g4.reshape(dbatch, dseq, N_KV, HPG, 3)
            gates_c = jnp.transpose(g4, (0, 3, 2, 1, 4)).reshape(dbatch, cols, 3)
            gates_c = jnp.pad(gates_c, ((0, 0), (0, 0), (0, LANES - 3)))
            sel_pad = jnp.pad(sel_new, ((0, 0), (0, PAGE_SIZE - dseq), (0, 0)))
            win_all = jnp.concatenate([state_win_kv[l].reshape(dbatch, w_buf, PAIR_W), win_new], axis=1)
            win_pad = jnp.pad(win_all, ((0, 0), (0, n_win_pad - w_buf - dseq), (0, 0)))
            o_q = _nsa_sample_sel(cache_sel_t, l, page_table, sel_pad, win_pad, qp, sel_mask, oc_t,
                                  gates_c, bias_s_s, bias_new_s, bias_w_s, t_pos_col)
            o6 = o_q.reshape(dbatch, HPG, N_KV, dseq, N_KV, HEAD_DIM)
            o = jnp.einsum('bhgtgd->btghd', o6).reshape(m_s, N_HEADS * HEAD_DIM)
            r, s_new = _retention(rqk, rv, gts, gn_w, gn_b, state_ret[l], *dec_s, dbatch, dseq, dseq, f32)
            kv6 = lambda a: a.reshape(dbatch, -1, 2, N_KV, HEAD_DIM)
            return o, r, (kv6(cmp_new), kv6(sel_new), kv6(win_all[:, -w_buf:]), s_new)

        rope_s = (cos_s, sin_s, pl.BlockSpec((tm_s, RET_DK // 2), lambda i, j: (i, 0)))
        xs, st = _layer(xs, mod_s, lw, dseq, tm_s, f32, False, rope_s, attend_sample)
        st_s.append(st)

    y_p = _norm("norm_final", xp, norm_final, None, None, seq, f32).reshape(batch, seq, D_MODEL)
    y_s = _norm("norm_final", xs, norm_final, None, None, dseq, f32).reshape(dbatch, dseq, D_MODEL)
    cmp_p, sel_p, win_p, ret_p = [jnp.stack([st[i] for st in st_p]) for i in range(4)]
    cmp_s, sel_s, win_s, ret_s = [jnp.stack([st[i] for st in st_s]) for i in range(4)]
    return (y_p, y_s, cmp_p, sel_p, win_p, ret_p, cmp_s, sel_s, win_s, ret_s)
```

```python
import functools
import math

import jax
import jax.numpy as jnp
from jax import lax
from jax.experimental import pallas as pl
from jax.experimental.pallas import tpu as pltpu

D_MODEL = 2048
DEPTH = 4
PAST_LEN = 16384
PAGE_SIZE = 128
N_HEADS = 16
HEAD_DIM = 64
N_KV = 4
HPG = N_HEADS // N_KV
CMP_BLOCK = 32
SEL_BLOCK = 64
TOP_K = 16
WINDOW = 512
RET_HEADS = 4
RET_DK = 256
RET_DV = 512
RET_CHUNK = 128
ROPE_BASE = 10000.0
D_FF = 4 * D_MODEL
N_BUCKETS = 32
REL_MAX_DIST = 2048
NORM_EPS = 1e-6
FORCE_BONUS = 1e6
NEG = -1e30

KV_W = N_KV * HEAD_DIM
PAIR_W = 2 * KV_W
NG_SLOT = 16
NG_W = 128

LANES = 128
SUBLANES = 8
VMEM_LIMIT_BYTES = 56 * 1024 * 1024

QB = 128
KB = 128
N_BIAS_TILES = 14
SEL_SUB = 4
LINEAR_TN = 1024
LINEAR_TK = 2048
LOG2E = math.log2(math.e)

f32 = jnp.float32
bf16 = jnp.bfloat16


def _cparams(sem):
    return pltpu.CompilerParams(dimension_semantics=sem, vmem_limit_bytes=VMEM_LIMIT_BYTES)


def _linear(name, xs, ws, extras, extra_specs, epilogue, out_shapes, out_specs, tm, tn, tk=None):
    m = xs[0].shape[0]
    n = ws[0].shape[1]
    nx, ne = len(xs), len(extras)
    if tk is None:
        def body(*refs):
            x_refs, w_refs = refs[:nx], refs[nx:2 * nx]
            e_refs = refs[2 * nx:2 * nx + ne]
            o_refs = refs[2 * nx + ne:]
            accs = [jnp.dot(x[...].astype(bf16), w[...], preferred_element_type=f32)
                    for x, w in zip(x_refs, w_refs)]
            epilogue(accs, e_refs, o_refs)

        in_specs = ([pl.BlockSpec((tm, x.shape[1]), lambda i, j: (i, 0)) for x in xs]
                    + [pl.BlockSpec((w.shape[0], tn), lambda i, j: (0, j)) for w in ws]
                    + list(extra_specs))
        return pl.pallas_call(
            body, grid=(m // tm, n // tn), in_specs=in_specs, out_specs=out_specs,
            out_shape=out_shapes, compiler_params=_cparams(("parallel", "arbitrary")),
            name=name)(*xs, *ws, *extras)

    assert nx == 1
    nk = xs[0].shape[1] // tk

    def body_k(x_ref, w_ref, *rest):
        e_refs, o_refs, acc_ref = rest[:ne], rest[ne:-1], rest[-1]
        k = pl.program_id(2)
        part = jnp.dot(x_ref[...].astype(bf16), w_ref[...], preferred_element_type=f32)

        @pl.when(k == 0)
        def _():
            acc_ref[...] = part

        @pl.when(k > 0)
        def _():
            acc_ref[...] += part

        @pl.when(k == nk - 1)
        def _():
            epilogue([acc_ref[...]], e_refs, o_refs)

    ij = lambda spec: pl.BlockSpec(spec.block_shape, lambda i, j, k, f=spec.index_map: f(i, j))
    single = not isinstance(out_specs, (list, tuple))
    outs = ij(out_specs) if single else [ij(s) for s in out_specs]
    in_specs = ([pl.BlockSpec((tm, tk), lambda i, j, k: (i, k)), pl.BlockSpec((tk, tn), lambda i, j, k: (k, j))]
                + [ij(s) for s in extra_specs])
    return pl.pallas_call(
        body_k, grid=(m // tm, n // tn, nk), in_specs=in_specs, out_specs=outs, out_shape=out_shapes,
        scratch_shapes=[pltpu.VMEM((tm, tn), f32)],
        compiler_params=_cparams(("parallel", "arbitrary", "arbitrary")), name=name)(*xs, *ws, *extras)


def _tile_spec(tm, tn, col0=0):
    return pl.BlockSpec((tm, tn), lambda i, j: (i, j + col0))


def _mod_spec(arr, tm, tn, rows_per_batch):
    if arr.ndim == 3:
        return pl.BlockSpec((None, 1, tn), lambda i, j: ((i * tm) // rows_per_batch, 0, j))
    return pl.BlockSpec((tm, tn), lambda i, j: (i, j))


def _mod_all(c_all, w_mod, b_mod):
    depth, d, n = w_mod.shape
    tn = 1024

    def body(c_ref, w_ref, b_ref, o_ref):
        c = c_ref[...]
        a = (c * jax.nn.sigmoid(c)).astype(bf16)
        o_ref[...] = jnp.dot(a, w_ref[...].astype(bf16), preferred_element_type=f32) + b_ref[...]

    return pl.pallas_call(
        body, grid=(depth, n // tn),
        in_specs=[pl.BlockSpec(c_all.shape, lambda l, j: (0, 0)),
                  pl.BlockSpec((None, d, tn), lambda l, j: (l, 0, j)),
                  pl.BlockSpec((None, 1, tn), lambda l, j: (l, 0, j))],
        out_specs=pl.BlockSpec((None, c_all.shape[0], tn), lambda l, j: (l, 0, j)),
        out_shape=jax.ShapeDtypeStruct((depth, c_all.shape[0], n), f32),
        compiler_params=_cparams(("parallel", "parallel")), name="mod_all",
    )(c_all, w_mod, b_mod.reshape(depth, 1, n))


def _norm(name, x, gain, sc, sh, rows_per_batch, out_dtype):
    m, d = x.shape
    tm = min(256, m)
    has_mod = sc is not None

    def body(*refs):
        if has_mod:
            x_ref, g_ref, sc_ref, sh_ref, o_ref = refs
        else:
            x_ref, g_ref, o_ref = refs
        xf = x_ref[...]
        y = xf * lax.rsqrt(jnp.mean(xf * xf, axis=-1, keepdims=True) + NORM_EPS) * g_ref[...]
        if has_mod:
            y = y * (1.0 + sc_ref[...]) + sh_ref[...]
        o_ref[...] = y.astype(out_dtype)

    in_specs = [pl.BlockSpec((tm, d), lambda i: (i, 0)), pl.BlockSpec((1, d), lambda i: (0, 0))]
    args = [x, gain.reshape(1, d)]
    if has_mod:
        for a in (sc, sh):
            if a.ndim == 3:
                in_specs.append(pl.BlockSpec((None, 1, d), lambda i: ((i * tm) // rows_per_batch, 0, 0)))
            else:
                in_specs.append(pl.BlockSpec((tm, d), lambda i: (i, 0)))
            args.append(a)
    return pl.pallas_call(
        body, grid=(m // tm,), in_specs=in_specs,
        out_specs=pl.BlockSpec((tm, d), lambda i: (i, 0)),
        out_shape=jax.ShapeDtypeStruct((m, d), out_dtype),
        compiler_params=_cparams(("parallel",)), name=name)(*args)


def _t5_bucket(dist):
    n = jnp.maximum(dist, 0)
    exact = N_BUCKETS // 2
    nf = jnp.maximum(n, exact).astype(f32)
    large = exact + (jnp.log(nf / exact) / math.log(REL_MAX_DIST / exact) * (N_BUCKETS - exact)).astype(jnp.int32)
    large = jnp.minimum(large, N_BUCKETS - 1)
    return jnp.where(n < exact, n, large)


def _bias_from_dist(dist, rel_bias, scale=1.0):
    r, c = dist.shape
    tr = next(t for t in (512, 256, 128, 64, 32, 16, 8) if r % t == 0)

    def body(tab_ref, d_ref, o_ref):
        h = pl.program_id(0)
        bucket = _t5_bucket(d_ref[...])
        acc = jnp.zeros(bucket.shape, f32)
        for k in range(N_BUCKETS):
            acc = jnp.where(bucket == k, tab_ref[k, h], acc)
        o_ref[...] = acc * scale

    return pl.pallas_call(
        body, grid=(N_HEADS, r // tr),
        in_specs=[pl.BlockSpec(memory_space=pltpu.SMEM),
                  pl.BlockSpec((tr, c), lambda h, i: (i, 0))],
        out_specs=pl.BlockSpec((None, tr, c), lambda h, i: (h, i, 0)),
        out_shape=jax.ShapeDtypeStruct((N_HEADS, r, c), f32),
        compiler_params=_cparams(("parallel", "parallel")), name="t5_bias",
    )(rel_bias, dist)


def _compress_accumulate(load_rows, pos_ref, wk_ref, wv_ref, rows):
    acc_k = jnp.zeros((rows, KV_W), f32)
    acc_v = jnp.zeros((rows, KV_W), f32)
    for c in range(CMP_BLOCK):
        lo = c * PAIR_W
        xk = (load_rows(c, 0) + pos_ref[:, lo:lo + KV_W]).astype(bf16)
        xv = (load_rows(c, 1) + pos_ref[:, lo + KV_W:lo + PAIR_W]).astype(bf16)
        acc_k = acc_k + jnp.dot(xk, wk_ref[c], preferred_element_type=f32)
        acc_v = acc_v + jnp.dot(xv, wv_ref[c], preferred_element_type=f32)
    return acc_k, acc_v


def _compress_prompt(kv_raw, pos_row, wk, wv, batch):
    rows_total = kv_raw.shape[1] // CMP_BLOCK
    n = rows_total // batch
    x = kv_raw.reshape(kv_raw.shape[0], rows_total, CMP_BLOCK * PAIR_W)

    def body(x_ref, pos_ref, wk_ref, wv_ref, kc_ref, vct_ref):
        def load_rows(c, kv):
            lo = c * PAIR_W + kv * KV_W
            return x_ref[:, lo:lo + KV_W]
        acc_k, acc_v = _compress_accumulate(load_rows, pos_ref, wk_ref, wv_ref, n)
        kc_ref[...] = acc_k.astype(bf16)
        vct_ref[...] = acc_v.T.astype(bf16)

    return pl.pallas_call(
        body, grid=(batch,),
        in_specs=[pl.BlockSpec((None, n, CMP_BLOCK * PAIR_W), lambda b: (0, b, 0)),
                  pl.BlockSpec(pos_row.shape, lambda b: (0, 0)),
                  pl.BlockSpec(wk.shape, lambda b: (0, 0, 0)),
                  pl.BlockSpec(wv.shape, lambda b: (0, 0, 0))],
        out_specs=[pl.BlockSpec((None, n, KV_W), lambda b: (b, 0, 0)),
                   pl.BlockSpec((None, KV_W, n), lambda b: (b, 0, 0))],
        out_shape=[jax.ShapeDtypeStruct((batch, n, KV_W), bf16),
                   jax.ShapeDtypeStruct((batch, KV_W, n), bf16)],
        compiler_params=_cparams(("parallel",)), name="compress_prompt",
    )(x, pos_row, wk, wv)


CMP_PAGES_PER_STEP = 32


def _gather_pages(pt_ref, cache_ref, xbuf, sems, layer, pp, steps, n_steps):
    g = pl.program_id(0) * steps + pl.program_id(1)
    slot = g % 2

    def copies(step, into):
        row, first = step // steps, (step % steps) * pp
        return [pltpu.make_async_copy(cache_ref.at[layer, pt_ref[row, first + p]], xbuf.at[into, p],
                                      sems.at[into, p]) for p in range(pp)]

    @pl.when(g == 0)
    def _():
        for c in copies(g, slot):
            c.start()

    @pl.when(g + 1 < n_steps)
    def _():
        for c in copies(g + 1, 1 - slot):
            c.start()

    for c in copies(g, slot):
        c.wait()
    return slot


CMP_ROW_PITCH = CMP_BLOCK + SUBLANES


def _compress_sample(cache_t, layer, page_table, pos_row, wk, wv):
    batch, n_pages = page_table.shape
    blocks_per_page = PAGE_SIZE // CMP_BLOCK
    pp = CMP_PAGES_PER_STEP
    steps = n_pages // pp
    rows = pp * blocks_per_page
    n_lane_tiles = PAIR_W // LANES
    page_pitch = blocks_per_page * CMP_ROW_PITCH

    def body(pt_ref, cache_ref, pos_ref, wk_ref, wv_ref, kc_ref, vct_ref, xbuf, sems, tscr):
        slot = _gather_pages(pt_ref, cache_ref, xbuf, sems, layer, pp, steps, batch * steps)

        def untranspose(p, carry):
            base = pl.multiple_of(p * page_pitch, SUBLANES)
            for j in range(n_lane_tiles):
                tile = xbuf[slot, p, j * LANES:(j + 1) * LANES, :].T
                for n in range(blocks_per_page):
                    tscr[j, pl.ds(base + n * CMP_ROW_PITCH, CMP_BLOCK), :] = tile[n * CMP_BLOCK:(n + 1) * CMP_BLOCK]
            return carry
        lax.fori_loop(0, pp, untranspose, 0)

        def load_rows(c, kv):
            tiles = [tscr[kv * (KV_W // LANES) + j, pl.ds(c, rows, stride=CMP_ROW_PITCH), :]
                     for j in range(KV_W // LANES)]
            return jnp.concatenate(tiles, axis=1)
        acc_k, acc_v = _compress_accumulate(load_rows, pos_ref, wk_ref, wv_ref, rows)
        kc_ref[...] = acc_k.astype(bf16)
        vct_ref[...] = acc_v.T.astype(bf16)

    grid_spec = pltpu.PrefetchScalarGridSpec(
        num_scalar_prefetch=1, grid=(batch, steps),
        in_specs=[pl.BlockSpec(memory_space=pl.ANY),
                  pl.BlockSpec(pos_row.shape, lambda b, s, pt: (0, 0)),
                  pl.BlockSpec(wk.shape, lambda b, s, pt: (0, 0, 0)),
                  pl.BlockSpec(wv.shape, lambda b, s, pt: (0, 0, 0))],
        out_specs=[pl.BlockSpec((None, rows, KV_W), lambda b, s, pt: (b, s, 0)),
                   pl.BlockSpec((None, KV_W, rows), lambda b, s, pt: (b, 0, s))],
        scratch_shapes=[pltpu.VMEM((2, pp, PAIR_W, PAGE_SIZE), f32), pltpu.SemaphoreType.DMA((2, pp)),
                        pltpu.VMEM((n_lane_tiles, pp * page_pitch, LANES), f32)])
    n_cmp = n_pages * blocks_per_page
    return pl.pallas_call(
        body, grid_spec=grid_spec,
        out_shape=[jax.ShapeDtypeStruct((batch, n_cmp, KV_W), bf16),
                   jax.ShapeDtypeStruct((batch, KV_W, n_cmp), bf16)],
        compiler_params=_cparams(("arbitrary", "arbitrary")), name="compress_sample",
    )(page_table, cache_t, pos_row, wk, wv)


def _softmax_cols(s, mask, exp=jnp.exp):
    sm = jnp.where(mask, s, NEG)
    m = jnp.max(sm, axis=0, keepdims=True)
    e = exp(sm - m)
    p = e / jnp.sum(e, axis=0, keepdims=True)
    return jnp.where(mask, p, 0.0)


def _topk_mask(score, score_scr, n_rows, k, groups):
    n_pad, c = score.shape
    per = -(-n_rows // groups)
    assert groups * per <= n_pad
    rows = lax.broadcasted_iota(jnp.int32, score.shape, 0)
    lane_grp = lax.broadcasted_iota(jnp.int32, (1, c), 1) // (c // groups)
    score_scr[...] = score

    def step(j, cnt):
        sj = score_scr[pl.ds(j, 1), :]
        for h in range(1, groups):
            sj = jnp.where(lane_grp == h, score_scr[pl.ds(j + h * per, 1), :], sj)
        jv = j + per * lane_grp
        ahead = (sj > score) | ((sj == score) & (jv < rows))
        return cnt + jnp.where(ahead, 1.0, 0.0)

    cnt = lax.fori_loop(0, per, step, jnp.zeros(score.shape, f32))
    total = cnt
    for h in range(1, groups):
        total = total + pltpu.roll(cnt, h * (c // groups), axis=1)
    return jnp.where(total < k, 1.0, 0.0)


def _topk_mask_unrolled(score, score_scr, k):
    n_rows, c = score.shape
    score_scr[...] = score
    n_tiles = n_rows // SUBLANES
    tiles = [score[SUBLANES * v:SUBLANES * (v + 1)] for v in range(n_tiles)]
    row_in_tile = lax.broadcasted_iota(jnp.int32, (SUBLANES, c), 0)
    cnt = [jnp.zeros((SUBLANES, c), f32) for _ in range(n_tiles)]
    for j in range(n_rows):
        sj = score_scr[j:j + 1, :]
        vj = j // SUBLANES
        for v in range(n_tiles):
            if v > vj:
                ahead = sj >= tiles[v]
            elif v < vj:
                ahead = sj > tiles[v]
            else:
                ahead = (sj > tiles[v]) | ((sj == tiles[v]) & (row_in_tile > j % SUBLANES))
            cnt[v] = cnt[v] + jnp.where(ahead, 1.0, 0.0)
    return jnp.concatenate([jnp.where(cv < k, 1.0, 0.0) for cv in cnt], axis=0)


def _nsa_prompt(q, kc, vct, k_rows, vt_rows, gates, bias_c, bias_t, batch, seq):
    nq = seq // QB
    n_cmp = seq // CMP_BLOCK
    n_sel = seq // SEL_BLOCK
    ratio = SEL_BLOCK // CMP_BLOCK
    k_eff = min(TOP_K, n_sel)
    wq = HPG * QB
    win_chunks = WINDOW // KB

    def body(q_ref, kc_ref, vct_ref, ks_ref, vst_ref, kw_ref, vwt_ref, g_ref, bc_ref, bt_ref,
             o_ref, imp_scr, score_scr, sel_scr, gate_scr):
        g = pl.program_id(1)
        i = pl.program_id(2)
        q0 = i * QB

        qt = q_ref[...].astype(f32).T
        qcat = jnp.concatenate([qt[h * HEAD_DIM:(h + 1) * HEAD_DIM] for h in range(HPG)], axis=1)
        qrep = jnp.concatenate([qcat] * N_KV, axis=0)
        row_grp = lax.broadcasted_iota(jnp.int32, (KV_W, wq), 0) // HEAD_DIM
        qpad = jnp.where(row_grp == g, qrep, 0.0).astype(bf16)

        t_row = q0 + lax.broadcasted_iota(jnp.int32, (1, wq), 1) % QB

        s_c = jnp.dot(kc_ref[...], qpad, preferred_element_type=f32) + bc_ref[...]
        c_end = lax.broadcasted_iota(jnp.int32, (n_cmp, wq), 0) * CMP_BLOCK + (CMP_BLOCK - 1)
        p_c = _softmax_cols(s_c, c_end <= t_row, jnp.exp2)
        o_c = jnp.dot(vct_ref[...], p_c.astype(bf16), preferred_element_type=f32)

        imp_scr[...] = sum(p_c[:, h * QB:(h + 1) * QB] for h in range(HPG))
        imp = sum(imp_scr[pl.ds(r, n_sel, stride=ratio), :] for r in range(ratio))
        blk = lax.broadcasted_iota(jnp.int32, (n_sel, QB), 0)
        t_q = q0 + lax.broadcasted_iota(jnp.int32, (n_sel, QB), 1)
        cur = t_q // SEL_BLOCK
        forced = (blk == 0) | (blk == cur) | (blk == cur - 1)
        score = imp + jnp.where(forced, FORCE_BONUS, 0.0)
        score = jnp.where(blk * SEL_BLOCK <= t_q, score, NEG)
        sel_scr[...] = _topk_mask_unrolled(score, score_scr, k_eff)

        kj = lax.broadcasted_iota(jnp.int32, (KB, QB), 0)
        qi = lax.broadcasted_iota(jnp.int32, (KB, QB), 1)
        rel = qi - kj
        upper = kj < SEL_BLOCK

        def scores(k_ref, chunk, bias, mask):
            k0 = pl.multiple_of(chunk * KB, KB)
            s = jnp.dot(k_ref[pl.ds(k0, KB), :], qpad, preferred_element_type=f32)
            penalty = jnp.where(mask, 0.0, NEG)
            return s + bias + jnp.concatenate([penalty] * HPG, axis=1)

        def weighted_values(vt_ref, chunks, ps):
            return sum(jnp.dot(vt_ref[:, pl.ds(pl.multiple_of(c * KB, KB), KB)], p.astype(bf16),
                               preferred_element_type=f32) for c, p in zip(chunks, ps))

        def col_max(ss):
            return functools.reduce(jnp.maximum, [jnp.max(s, axis=0, keepdims=True) for s in ss])

        def sel_step(j, carry):
            m, l, acc = carry
            chunks, ss = [], []
            for r in range(SEL_SUB):
                c = j * SEL_SUB + r
                cc = jnp.minimum(c, i)
                rows2 = sel_scr[pl.ds(pl.multiple_of(cc * (KB // SEL_BLOCK), KB // SEL_BLOCK),
                                      KB // SEL_BLOCK), :]
                chosen = jnp.where(upper, rows2[0:1, :], rows2[1:2, :]) > 0.5
                mask = chosen & (rel + (i - c) * KB >= 0)
                ss.append(scores(ks_ref, cc, bt_ref[jnp.minimum(i - cc, N_BIAS_TILES - 1)], mask))
                chunks.append(cc)
            m_new = jnp.maximum(m, col_max(ss))
            alpha = jnp.exp2(m - m_new)
            ps = [jnp.exp2(s - m_new) for s in ss]
            l = alpha * l + sum(jnp.sum(p, axis=0, keepdims=True) for p in ps)
            acc = alpha * acc + weighted_values(vst_ref, chunks, ps)
            return m_new, l, acc

        init = (jnp.full((1, wq), NEG, f32), jnp.zeros((1, wq), f32), jnp.zeros((HEAD_DIM, wq), f32))
        _, l_s, acc_s = lax.fori_loop(0, (i + SEL_SUB) // SEL_SUB, sel_step, init)
        o_s = acc_s / l_s

        chunks, ss = [], []
        for dd in range(win_chunks + 1):
            c = i - dd
            dist = rel + (dd * KB + jnp.where(c >= 0, 0, 2 * WINDOW))
            mask = (dist >= 0) & (dist < WINDOW)
            chunks.append(jnp.maximum(c, 0))
            ss.append(scores(kw_ref, chunks[-1], bt_ref[dd], mask))
        m_w = col_max(ss)
        ps = [jnp.exp2(s - m_w) for s in ss]
        l_w = sum(jnp.sum(p, axis=0, keepdims=True) for p in ps)
        o_w = weighted_values(vwt_ref, chunks, ps) / l_w

        gate_scr[...] = g_ref[...].T
        gg = gate_scr[pl.ds(pl.multiple_of(g * NG_SLOT, NG_SLOT), NG_SLOT), :]
        outs = []
        for h in range(HPG):
            sl = slice(h * QB, (h + 1) * QB)
            outs.append(gg[3 * h:3 * h + 1] * o_c[:, sl] + gg[3 * h + 1:3 * h + 2] * o_s[:, sl]
                        + gg[3 * h + 2:3 * h + 3] * o_w[:, sl])
        o_ref[...] = jnp.concatenate(outs, axis=0).T.astype(o_ref.dtype)

    return pl.pallas_call(
        body, grid=(batch, N_KV, nq),
        in_specs=[
            pl.BlockSpec((QB, KV_W), lambda b, g, i: (b * nq + i, g)),
            pl.BlockSpec((None, n_cmp, KV_W), lambda b, g, i: (b, 0, 0)),
            pl.BlockSpec((None, HEAD_DIM, n_cmp), lambda b, g, i: (b, g, 0)),
            pl.BlockSpec((None, seq, KV_W), lambda b, g, i: (1, b, 0)),
            pl.BlockSpec((None, HEAD_DIM, seq), lambda b, g, i: (1, g, b)),
            pl.BlockSpec((None, seq, KV_W), lambda b, g, i: (2, b, 0)),
            pl.BlockSpec((None, HEAD_DIM, seq), lambda b, g, i: (2, g, b)),
            pl.BlockSpec((QB, NG_W), lambda b, g, i: (b * nq + i, 0)),
            pl.BlockSpec((None, None, n_cmp, wq), lambda b, g, i: (i, g, 0, 0)),
            pl.BlockSpec((N_BIAS_TILES, None, KB, wq), lambda b, g, i: (0, g, 0, 0)),
        ],
        out_specs=pl.BlockSpec((QB, KV_W), lambda b, g, i: (b * nq + i, g)),
        out_shape=jax.ShapeDtypeStruct((batch * seq, N_HEADS * HEAD_DIM), bf16),
        scratch_shapes=[pltpu.VMEM((n_cmp, QB), f32), pltpu.VMEM((n_sel, QB), f32),
                        pltpu.VMEM((n_sel, QB), f32), pltpu.VMEM((NG_W, QB), f32)],
        compiler_params=_cparams(("parallel", "parallel", "arbitrary")), name="nsa_prompt",
    )(q, kc, vct, k_rows, vt_rows, k_rows, vt_rows, gates, bias_c, bias_t)


def _retention(qk, v, rg, gn_w, gn_b, state0, dec_in, dec_q, dec_k, dec_c, batch, length, chunk, out_dtype):
    n = length // chunk
    w_qk, w_v = RET_HEADS * RET_DK, RET_HEADS * RET_DV

    def body(q_ref, k_ref, v_ref, rg_ref, gw_ref, gb_ref, s0_ref, din_ref, dq_ref, dk_ref, dc_ref,
             r_ref, s_ref):
        @pl.when(pl.program_id(1) == 0)
        def _():
            s_ref[...] = s0_ref[...]

        for h in range(RET_HEADS):
            qs, vs = slice(h * RET_DK, (h + 1) * RET_DK), slice(h * RET_DV, (h + 1) * RET_DV)
            qc = q_ref[:, qs].astype(bf16)
            kf = k_ref[:, qs].astype(f32)
            vc = v_ref[:, vs].astype(bf16)
            st = s_ref[h]
            inner = lax.dot_general(qc, kf.astype(bf16), (((1,), (1,)), ((), ())),
                                    preferred_element_type=f32) * din_ref[h]
            o = jnp.dot(inner.astype(bf16), vc, preferred_element_type=f32)
            o = o + jnp.dot(qc, st.astype(bf16), preferred_element_type=f32) * dq_ref[h]
            kd = (kf * dk_ref[h]).astype(bf16)
            s_ref[h] = st * dc_ref[h] + lax.dot_general(kd, vc, (((0,), (0,)), ((), ())),
                                                        preferred_element_type=f32)
            mu = jnp.mean(o, axis=-1, keepdims=True)
            var = jnp.mean(jnp.square(o - mu), axis=-1, keepdims=True)
            y = (o - mu) * lax.rsqrt(var + NORM_EPS) * gw_ref[:, vs] + gb_ref[:, vs]
            gate = rg_ref[:, vs]
            r_ref[:, vs] = (gate * jax.nn.sigmoid(gate) * y).astype(r_ref.dtype)

    seq_spec = lambda w, col0=0: pl.BlockSpec((chunk, w), lambda b, c: (b * n + c, col0))
    full_spec = lambda a: pl.BlockSpec(a.shape, lambda b, c: (0,) * a.ndim)
    state_spec = pl.BlockSpec((None, RET_HEADS, RET_DK, RET_DV), lambda b, c: (b, 0, 0, 0))
    return pl.pallas_call(
        body, grid=(batch, n),
        in_specs=[seq_spec(w_qk), seq_spec(w_qk, 1), seq_spec(w_v), seq_spec(w_v),
                  full_spec(gn_w), full_spec(gn_b), state_spec,
                  full_spec(dec_in), full_spec(dec_q), full_spec(dec_k), full_spec(dec_c)],
        out_specs=[seq_spec(w_v), state_spec],
        out_shape=[jax.ShapeDtypeStruct((batch * length, w_v), out_dtype),
                   jax.ShapeDtypeStruct((batch, RET_HEADS, RET_DK, RET_DV), f32)],
        compiler_params=_cparams(("parallel", "arbitrary")), name="retention",
    )(qk, qk, v, rg, gn_w, gn_b, state0, dec_in, dec_q, dec_k, dec_c)


def _retention_decays(chunk):
    lg = jnp.log(1.0 - jnp.exp2(-5.0 - jnp.arange(RET_HEADS, dtype=f32)))
    i = jnp.arange(chunk, dtype=f32)
    diff = i[:, None] - i[None, :]
    dec_in = jnp.where(diff >= 0, jnp.exp(jnp.maximum(diff, 0.0)[None] * lg[:, None, None]), 0.0)
    dec_q = jnp.exp((i[None, :] + 1.0) * lg[:, None])[..., None]
    dec_k = jnp.exp((chunk - 1.0 - i)[None, :] * lg[:, None])[..., None]
    dec_c = jnp.exp(chunk * lg)[:, None, None]
    return dec_in, dec_q, dec_k, dec_c


def _rope_tables(pos):
    half = RET_DK // 2
    inv = ROPE_BASE ** (-jnp.arange(half, dtype=f32) / half)
    ang = pos.astype(f32)[:, None] * inv[None, :]
    return jnp.cos(ang), jnp.sin(ang)


def _nsa_sample_cmp(kc, vct, qpt, bias_c, t_pos_row, n_sel, n_sel_pad):
    batch, n_cmp, _ = kc.shape
    ratio = SEL_BLOCK // CMP_BLOCK
    cols = qpt.shape[2]
    grp_cols = cols // HPG

    def body(kc_ref, vct_ref, q_ref, bc_ref, t_ref, oc_ref, sel_ref, imp_scr, score_scr):
        t_row = t_ref[...]
        s_c = jnp.dot(kc_ref[...], q_ref[...], preferred_element_type=f32) + bc_ref[...]
        c_end = lax.broadcasted_iota(jnp.int32, (n_cmp, cols), 0) * CMP_BLOCK + (CMP_BLOCK - 1)
        p_c = _softmax_cols(s_c, c_end <= t_row)
        oc_ref[...] = jnp.dot(vct_ref[...], p_c.astype(bf16), preferred_element_type=f32)
        assert HPG == 4
        pair = p_c + pltpu.roll(p_c, 2 * grp_cols, axis=1)
        imp_rep = pair + pltpu.roll(pair, grp_cols, axis=1)
        imp_scr[...] = jnp.zeros(imp_scr.shape, f32)
        imp_scr[pl.ds(0, n_cmp), :] = imp_rep
        imp = sum(imp_scr[pl.ds(r, n_sel_pad, stride=ratio), :] for r in range(ratio))
        blk = lax.broadcasted_iota(jnp.int32, (n_sel_pad, cols), 0)
        cur = t_row // SEL_BLOCK
        forced = (blk == 0) | (blk == cur) | (blk == cur - 1)
        score = imp + jnp.where(forced, FORCE_BONUS, 0.0)
        score = jnp.where(blk * SEL_BLOCK <= t_row, score, NEG)
        score = jnp.where(blk < n_sel, score, -jnp.inf)
        sel_ref[...] = _topk_mask(score, score_scr, n_sel, min(TOP_K, n_sel), HPG)

    return pl.pallas_call(
        body, grid=(batch,),
        in_specs=[pl.BlockSpec((None, n_cmp, KV_W), lambda b: (b, 0, 0)),
                  pl.BlockSpec((None, KV_W, n_cmp), lambda b: (b, 0, 0)),
                  pl.BlockSpec((None, KV_W, cols), lambda b: (b, 0, 0)),
                  pl.BlockSpec((n_cmp, cols), lambda b: (0, 0)),
                  pl.BlockSpec((1, cols), lambda b: (0, 0))],
        out_specs=[pl.BlockSpec((None, KV_W, cols), lambda b: (b, 0, 0)),
                   pl.BlockSpec((None, n_sel_pad, cols), lambda b: (b, 0, 0))],
        out_shape=[jax.ShapeDtypeStruct((batch, KV_W, cols), f32),
                   jax.ShapeDtypeStruct((batch, n_sel_pad, cols), f32)],
        scratch_shapes=[pltpu.VMEM((n_sel_pad * ratio, cols), f32), pltpu.VMEM((n_sel_pad, cols), f32)],
        compiler_params=_cparams(("parallel",)), name="nsa_sample_cmp",
    )(kc, vct, qpt, bias_c, t_pos_row)


SEL_PAGES_PER_STEP = 16


def _softmax_rows(s, mask):
    sm = jnp.where(mask, s, NEG)
    m = jnp.max(sm, axis=1, keepdims=True)
    e = jnp.exp(sm - m)
    p = e / jnp.sum(e, axis=1, keepdims=True)
    return jnp.where(mask, p, 0.0)


def _nsa_sample_sel(cache_t, layer, page_table, sel_new, win_all, qp, sel_mask, oc_t, gates_c,
                    bias_s, bias_new, bias_w, t_pos_col):
    batch, n_pages = page_table.shape
    pp = SEL_PAGES_PER_STEP
    steps = n_pages // pp
    keys = pp * PAGE_SIZE
    nq = qp.shape[1]
    blocks_per_page = PAGE_SIZE // SEL_BLOCK
    nb = pp * blocks_per_page
    n_win_pad = win_all.shape[1]
    w_buf = min(WINDOW, PAST_LEN)
    nt = (((1,), (1,)), ((), ()))
    tn = (((0,), (0,)), ((), ()))

    def body(pt_ref, cache_ref, new_ref, win_ref, q_ref, selm_ref, oc_ref, g_ref, bs_ref, bn_ref, bw_ref,
             t_ref, o_ref, xbuf, sems, m_scr, l_scr, acc_scr):
        s = pl.program_id(1)
        slot = _gather_pages(pt_ref, cache_ref, xbuf, sems, layer, pp, steps, batch * steps)

        @pl.when(s == 0)
        def _():
            m_scr[...] = jnp.full(m_scr.shape, NEG, f32)
            l_scr[...] = jnp.zeros(l_scr.shape, f32)
            acc_scr[...] = jnp.zeros(acc_scr.shape, f32)

        q = q_ref[...]
        t_col = t_ref[...]

        def chosen_keys(block_rows, n_keys):
            n_blocks = block_rows.shape[0]
            key_blk = lax.broadcasted_iota(jnp.int32, (n_blocks, n_keys), 1) // SEL_BLOCK
            spread = jnp.where(key_blk == lax.broadcasted_iota(jnp.int32, (n_blocks, n_keys), 0), 1.0, 0.0)
            return lax.dot_general(block_rows, spread, tn, preferred_element_type=f32) > 0.5

        def online(s_qk, mask, weighted_values):
            s_m = jnp.where(mask, s_qk, NEG)
            m_old = m_scr[...]
            m_new = jnp.maximum(m_old, jnp.max(s_m, axis=1, keepdims=True))
            alpha = jnp.exp(m_old - m_new)
            p = jnp.where(mask, jnp.exp(s_m - m_new), 0.0)
            l_scr[...] = alpha * l_scr[...] + jnp.sum(p, axis=1, keepdims=True)
            acc_scr[...] = alpha * acc_scr[...] + weighted_values(p.astype(bf16))
            m_scr[...] = m_new

        s_qk = jnp.concatenate(
            [jnp.dot(q, xbuf[slot, p, :KV_W, :].astype(bf16), preferred_element_type=f32) for p in range(pp)],
            axis=1) + bs_ref[...]
        mask = chosen_keys(selm_ref[pl.ds(pl.multiple_of(s * nb, nb), nb), :], keys)
        online(s_qk, mask, lambda pb: sum(
            lax.dot_general(pb[:, p * PAGE_SIZE:(p + 1) * PAGE_SIZE], xbuf[slot, p, KV_W:, :].astype(bf16), nt,
                            preferred_element_type=f32) for p in range(pp)))

        @pl.when(s == steps - 1)
        def _():
            kv_n = new_ref[...]
            s_n = lax.dot_general(q, kv_n[:, :KV_W].astype(bf16), nt, preferred_element_type=f32) + bn_ref[...]
            pos_n = PAST_LEN + lax.broadcasted_iota(jnp.int32, (nq, PAGE_SIZE), 1)
            mask_n = chosen_keys(selm_ref[pl.ds(n_pages * blocks_per_page, SUBLANES), :], PAGE_SIZE) & (pos_n <= t_col)
            online(s_n, mask_n, lambda pb: jnp.dot(pb, kv_n[:, KV_W:].astype(bf16), preferred_element_type=f32))
            o_s = acc_scr[...] / l_scr[...]

            kv_w = win_ref[...]
            kpos = (PAST_LEN - w_buf) + lax.broadcasted_iota(jnp.int32, (nq, n_win_pad), 1)
            dist = t_col - kpos
            mask_w = (dist >= 0) & (dist < WINDOW) & (kpos >= 0)
            s_w = lax.dot_general(q, kv_w[:, :KV_W].astype(bf16), nt, preferred_element_type=f32) + bw_ref[...]
            p_w = _softmax_rows(s_w, mask_w)
            o_w = jnp.dot(p_w.astype(bf16), kv_w[:, KV_W:].astype(bf16), preferred_element_type=f32)
            gts = g_ref[...]
            o_ref[...] = gts[:, 0:1] * oc_ref[...].T + gts[:, 1:2] * o_s + gts[:, 2:3] * o_w

    grid_spec = pltpu.PrefetchScalarGridSpec(
        num_scalar_prefetch=1, grid=(batch, steps),
        in_specs=[pl.BlockSpec(memory_space=pl.ANY),
                  pl.BlockSpec((None, PAGE_SIZE, PAIR_W), lambda b, s, pt: (b, 0, 0)),
                  pl.BlockSpec((None, n_win_pad, PAIR_W), lambda b, s, pt: (b, 0, 0)),
                  pl.BlockSpec((None, nq, KV_W), lambda b, s, pt: (b, 0, 0)),
                  pl.BlockSpec((None, sel_mask.shape[1], nq), lambda b, s, pt: (b, 0, 0)),
                  pl.BlockSpec((None, KV_W, nq), lambda b, s, pt: (b, 0, 0)),
                  pl.BlockSpec((None, nq, LANES), lambda b, s, pt: (b, 0, 0)),
                  pl.BlockSpec((nq, keys), lambda b, s, pt: (0, s)),
                  pl.BlockSpec((nq, PAGE_SIZE), lambda b, s, pt: (0, 0)),
                  pl.BlockSpec((nq, n_win_pad), lambda b, s, pt: (0, 0)),
                  pl.BlockSpec((nq, 1), lambda b, s, pt: (0, 0))],
        out_specs=pl.BlockSpec((None, nq, KV_W), lambda b, s, pt: (b, 0, 0)),
        scratch_shapes=[pltpu.VMEM((2, pp, PAIR_W, PAGE_SIZE), f32), pltpu.SemaphoreType.DMA((2, pp)),
                        pltpu.VMEM((nq, 1), f32), pltpu.VMEM((nq, 1), f32),
                        pltpu.VMEM((nq, KV_W), f32)])
    return pl.pallas_call(
        body, grid_spec=grid_spec,
        out_shape=jax.ShapeDtypeStruct((batch, nq, KV_W), f32),
        compiler_params=_cparams(("arbitrary", "arbitrary")), name="nsa_sample_sel",
    )(page_table, cache_t, sel_new, win_all, qp, sel_mask, oc_t, gates_c, bias_s, bias_new, bias_w,
      t_pos_col)


_OFF_Q = 0
_OFF_KV = N_HEADS * HEAD_DIM
_OFF_NG = _OFF_KV + 6 * KV_W
_OFF_RQ = _OFF_NG + 3 * N_HEADS
_OFF_RV = _OFF_RQ + 2 * RET_HEADS * RET_DK
_OFF_RG = _OFF_RV + RET_HEADS * RET_DV
_D_IN = _OFF_RG + RET_HEADS * RET_DV + 2 * D_MODEL


def _prep_w_in(w_in_l):
    w = w_in_l.astype(bf16)
    w_attn = w[:, :_OFF_NG]
    ng = w[:, _OFF_NG:_OFF_RQ].reshape(D_MODEL, N_KV, 3 * HPG)
    w_ng = jnp.pad(ng, ((0, 0), (0, 0), (0, NG_SLOT - 3 * HPG))).reshape(D_MODEL, N_KV * NG_SLOT)
    w_ng = jnp.pad(w_ng, ((0, 0), (0, NG_W - N_KV * NG_SLOT)))
    w_ret = w[:, _OFF_RQ:]
    return w_attn, w_ng, w_ret


def _prep_w_phi(w_phi_l, cmp_pos_l):
    eye = jnp.eye(N_KV, dtype=f32)
    wk = jnp.einsum('gh,cde->cgdhe', eye, w_phi_l[0]).reshape(CMP_BLOCK, KV_W, KV_W).astype(bf16)
    wv = jnp.einsum('gh,cde->cgdhe', eye, w_phi_l[1]).reshape(CMP_BLOCK, KV_W, KV_W).astype(bf16)
    pos = jnp.broadcast_to(cmp_pos_l[:, :, None, :], (2, CMP_BLOCK, N_KV, HEAD_DIM))
    pos_row = jnp.transpose(pos, (1, 0, 2, 3)).reshape(1, CMP_BLOCK * PAIR_W)
    return wk, wv, pos_row


def _layer(x, mod, lw, rows_per_batch, tm, act, q_scale, kv_layouts, rope, attend):
    m = x.shape[0]
    sh1, sc1, g1, sh2, sc2, g2 = mod
    h = _norm("norm_mix", x, lw["norm_mix"], sc1, sh1, rows_per_batch, act)

    def plain(dtype, scale=1.0):
        def epi(accs, e_refs, o_refs):
            o_refs[0][...] = (accs[0] * scale).astype(dtype)
        return epi

    w_attn, w_ng, w_ret = lw["w_attn"], lw["w_ng"], lw["w_ret"]
    tn = LINEAR_TN
    q = _linear("proj_q", [h], [w_attn[:, :_OFF_KV]], [], [], plain(act, q_scale),
                jax.ShapeDtypeStruct((m, _OFF_KV), act), _tile_spec(tm, tn), tm, tn)
    pair_spec = lambda w: pl.BlockSpec((None, tm, w), lambda i, j: (j, i, 0))
    if kv_layouts:
        def kv_epi(accs, e_refs, o_refs):
            a = accs[0]
            o_refs[0][...] = a
            o_refs[1][...] = a[:, :KV_W].astype(bf16)
            o_refs[2][...] = a[:, KV_W:].T.astype(bf16)
        kv_raw = _linear("proj_kv", [h], [w_attn[:, _OFF_KV:]], [], [], kv_epi,
                         [jax.ShapeDtypeStruct((3, m, PAIR_W), f32), jax.ShapeDtypeStruct((3, m, KV_W), bf16),
                          jax.ShapeDtypeStruct((3, KV_W, m), bf16)],
                         [pair_spec(PAIR_W), pair_spec(KV_W),
                          pl.BlockSpec((None, KV_W, tm), lambda i, j: (j, 0, i))], tm, PAIR_W)
    else:
        kv_raw = _linear("proj_kv", [h], [w_attn[:, _OFF_KV:]], [], [], plain(f32),
                         jax.ShapeDtypeStruct((3, m, PAIR_W), f32), pair_spec(PAIR_W), tm, PAIR_W)

    def sig_epi(accs, e_refs, o_refs):
        o_refs[0][...] = jax.nn.sigmoid(accs[0])
    gates = _linear("proj_ng", [h], [w_ng], [], [], sig_epi,
                    jax.ShapeDtypeStruct((m, NG_W), f32), _tile_spec(tm, NG_W), tm, NG_W)

    cos, sin, rope_spec = rope
    n_qk_tiles = RET_HEADS * RET_DK // tn

    def rot_epi(accs, e_refs, o_refs):
        a = accs[0]
        c, s = e_refs[0][...], e_refs[1][...]
        scale = jnp.where(pl.program_id(1) >= n_qk_tiles, RET_DK ** -0.5, 1.0)
        half = RET_DK // 2
        outs = []
        for hh in range(tn // RET_DK):
            x1 = a[:, hh * RET_DK:hh * RET_DK + half]
            x2 = a[:, hh * RET_DK + half:(hh + 1) * RET_DK]
            outs += [x1 * c - x2 * s, x2 * c + x1 * s]
        o_refs[0][...] = (jnp.concatenate(outs, axis=1) * scale).astype(act)

    n_rot = 2 * RET_HEADS * RET_DK
    rqk = _linear("proj_rqk", [h], [w_ret[:, :n_rot]], [cos, sin], [rope_spec, rope_spec], rot_epi,
                  jax.ShapeDtypeStruct((m, n_rot), act), _tile_spec(tm, tn), tm, tn)
    n_rv = RET_HEADS * RET_DV
    rv = _linear("proj_rv", [h], [w_ret[:, n_rot:n_rot + n_rv]], [], [], plain(act),
                 jax.ShapeDtypeStruct((m, n_rv), act), _tile_spec(tm, tn), tm, tn)
    gts = _linear("proj_gates", [h], [w_ret[:, n_rot + n_rv:]], [], [], plain(f32),
                  jax.ShapeDtypeStruct((m, 3 * D_MODEL), f32), _tile_spec(tm, tn), tm, tn)

    o_nsa, r, states = attend(q, kv_raw, gates, rqk, rv, gts)

    def merge_epi(accs, e_refs, o_refs):
        o_refs[0][...] = (jax.nn.sigmoid(e_refs[0][...]) * accs[0]
                          + jax.nn.sigmoid(e_refs[1][...]) * accs[1]).astype(act)
    nt = D_MODEL // tn
    merged = _linear("merge", [o_nsa, r], [lw["w_nsa_proj"], lw["w_ret_proj"]], [gts, gts],
                     [_tile_spec(tm, tn, nt), _tile_spec(tm, tn, 2 * nt)], merge_epi,
                     jax.ShapeDtypeStruct((m, D_MODEL), act), _tile_spec(tm, tn), tm, tn)

    def resid_epi(accs, e_refs, o_refs):
        o_refs[0][...] = e_refs[0][...] + e_refs[1][...] * accs[0]
    x = _linear("out_proj", [merged], [lw["w_out"]], [x, g1],
                [_tile_spec(tm, tn), _mod_spec(g1, tm, tn, rows_per_batch)], resid_epi,
                jax.ShapeDtypeStruct((m, D_MODEL), f32), _tile_spec(tm, tn), tm, tn)

    h2 = _norm("norm_mlp", x, lw["norm_mlp"], sc2, sh2, rows_per_batch, act)

    def up_epi(accs, e_refs, o_refs):
        o_refs[0][...] = jnp.square(jnp.maximum(accs[0], 0.0)).astype(act)
    u = _linear("mlp_up", [h2], [lw["w_up"]], [], [], up_epi,
                jax.ShapeDtypeStruct((m, D_FF), act), _tile_spec(tm, tn), tm, tn)
    x = _linear("mlp_down", [u], [lw["w_down"]], [x, g2],
                [_tile_spec(tm, tn), _mod_spec(g2, tm, tn, rows_per_batch)], resid_epi,
                jax.ShapeDtypeStruct((m, D_MODEL), f32), _tile_spec(tm, tn), tm, tn, tk=LINEAR_TK)
    return x, states


def kernel(x_prompt, x_sample, c_prompt, c_sample, cache_cmp_kv, cache_sel_kv, state_win_kv, state_ret,
           page_table, rel_bias, w_mod, b_mod, norm_mix, norm_mlp, w_in, cmp_pos, w_phi, w_nsa_proj,
           ret_gn_w, ret_gn_b, w_ret_proj, w_out, w_up, w_down, norm_final):
    batch, seq, _ = x_prompt.shape
    dbatch, dseq, _ = x_sample.shape
    depth = w_in.shape[0]
    n_pool = cache_cmp_kv.shape[1]
    n_pages = page_table.shape[1]
    i32 = jnp.int32

    n_c = batch + dbatch
    c_all = jnp.pad(jnp.concatenate([c_prompt, c_sample], axis=0), ((0, (-n_c) % SUBLANES), (0, 0)))
    mod_all = _mod_all(c_all, w_mod, b_mod)

    nq = seq // QB
    n_cmp_p = seq // CMP_BLOCK
    kj = jnp.arange(KB, dtype=i32)[:, None]
    qi = jnp.arange(QB, dtype=i32)[None, :]
    dist_tiles = (jnp.arange(N_BIAS_TILES, dtype=i32)[:, None, None] * KB + qi[None] - kj[None])
    bias_t = _bias_from_dist(dist_tiles.reshape(N_BIAS_TILES * KB, QB), rel_bias, LOG2E)
    bias_t = bias_t.reshape(N_KV, HPG, N_BIAS_TILES, KB, QB).transpose(2, 0, 3, 1, 4).reshape(
        N_BIAS_TILES, N_KV, KB, HPG * QB)
    t_all = jnp.arange(seq, dtype=i32)
    c_end_p = jnp.arange(n_cmp_p, dtype=i32) * CMP_BLOCK + (CMP_BLOCK - 1)
    bias_c = _bias_from_dist(t_all[None, :] - c_end_p[:, None], rel_bias, LOG2E)
    bias_c = bias_c.reshape(N_KV, HPG, n_cmp_p, nq, QB).transpose(3, 0, 2, 1, 4).reshape(
        nq, N_KV, n_cmp_p, HPG * QB)

    cos_p, sin_p = _rope_tables(jnp.arange(seq, dtype=i32))
    dec_p = _retention_decays(RET_CHUNK)

    cols = HPG * N_KV * dseq
    t_cols = PAST_LEN + jnp.tile(jnp.arange(dseq, dtype=i32), HPG * N_KV)
    head_cols = jnp.tile(jnp.repeat(jnp.arange(N_KV, dtype=i32), dseq), HPG) * HPG + jnp.repeat(
        jnp.arange(HPG, dtype=i32), N_KV * dseq)
    t_new = PAST_LEN + jnp.arange(dseq, dtype=i32)

    def col_bias(key_pos):
        per_head = _bias_from_dist(t_new[:, None] - key_pos[None, :], rel_bias)
        t_idx = jnp.tile(jnp.arange(dseq, dtype=i32), HPG * N_KV)
        return per_head[head_cols, t_idx, :]

    n_cmp_s = PAST_LEN // CMP_BLOCK
    bias_c_s = col_bias(jnp.arange(n_cmp_s, dtype=i32) * CMP_BLOCK + (CMP_BLOCK - 1)).T
    bias_s_s = col_bias(jnp.arange(PAST_LEN, dtype=i32))
    bias_new_s = col_bias(PAST_LEN + jnp.arange(PAGE_SIZE, dtype=i32))
    w_buf = state_win_kv.shape[2]
    n_win_pad = -(-(w_buf + dseq) // LANES) * LANES
    bias_w_s = col_bias(PAST_LEN - w_buf + jnp.arange(n_win_pad, dtype=i32))
    n_sel_s = -(-(PAST_LEN + dseq) // SEL_BLOCK)
    n_sel_pad = -(-(n_pages * (PAGE_SIZE // SEL_BLOCK) + PAGE_SIZE // SEL_BLOCK) // SUBLANES) * SUBLANES
    assert n_sel_pad >= n_sel_s
    cos_s, sin_s = _rope_tables(t_new)
    cos_s, sin_s = jnp.tile(cos_s, (dbatch, 1)), jnp.tile(sin_s, (dbatch, 1))
    dec_s = _retention_decays(dseq)
    t_pos_row = t_cols.reshape(1, cols)
    t_pos_col = t_cols.reshape(cols, 1)

    page_major = lambda c: jnp.transpose(c, (0, 1, 3, 4, 5, 2)).reshape(depth, n_pool, PAIR_W, PAGE_SIZE)
    cache_cmp_t = page_major(cache_cmp_kv)
    cache_sel_t = page_major(cache_sel_kv)

    m_p = batch * seq
    m_s = dbatch * dseq
    tm_p = 1024
    tm_s = m_s

    xp = x_prompt.reshape(m_p, D_MODEL)
    xs = x_sample.reshape(m_s, D_MODEL)
    st_p, st_s = [], []
    for l in range(depth):
        w_attn, w_ng, w_ret = _prep_w_in(w_in[l])
        wk, wv, pos_row = _prep_w_phi(w_phi[l], cmp_pos[l])
        lw = dict(norm_mix=norm_mix[l], norm_mlp=norm_mlp[l], w_attn=w_attn, w_ng=w_ng, w_ret=w_ret,
                  w_nsa_proj=w_nsa_proj[l].astype(bf16), w_ret_proj=w_ret_proj[l].astype(bf16),
                  w_out=w_out[l].astype(bf16), w_up=w_up[l].astype(bf16), w_down=w_down[l].astype(bf16))
        gn_w = ret_gn_w[l].reshape(1, -1)
        gn_b = ret_gn_b[l].reshape(1, -1)

        mod_p = [a.reshape(batch, 1, D_MODEL) for a in jnp.split(mod_all[l, :batch], 6, axis=-1)]

        def attend_prompt(q, kv, gates, rqk, rv, gts):
            kv_raw, k_rows, vt_rows = kv
            kc, vct = _compress_prompt(kv_raw, pos_row, wk, wv, batch)
            o = _nsa_prompt(q, kc, vct, k_rows, vt_rows, gates, bias_c, bias_t, batch, seq)
            zero_state = jnp.zeros((batch, RET_HEADS, RET_DK, RET_DV), f32)
            r, s_new = _retention(rqk, rv, gts, gn_w, gn_b, zero_state, *dec_p, batch, seq, RET_CHUNK, bf16)
            kv6 = lambda a: a.reshape(batch, seq, 2, N_KV, HEAD_DIM)
            win = kv6(kv_raw[2])[:, seq - min(WINDOW, seq):]
            return o, r, (kv6(kv_raw[0]), kv6(kv_raw[1]), win, s_new)

        rope_p = (cos_p, sin_p, pl.BlockSpec((tm_p, RET_DK // 2), lambda i, j: (i % (seq // tm_p), 0)))
        xp, st = _layer(xp, mod_p, lw, seq, tm_p, bf16, HEAD_DIM ** -0.5 * LOG2E, True, rope_p, attend_prompt)
        st_p.append(st)

        mod_rows = jnp.repeat(mod_all[l, batch:batch + dbatch], dseq, axis=0)
        mod_s = jnp.split(mod_rows, 6, axis=-1)

        def attend_sample(q, kv_raw, gates, rqk, rv, gts):
            cmp_new, sel_new, win_new = (kv_raw[j].reshape(dbatch, dseq, PAIR_W) for j in range(3))
            kc, vct = _compress_sample(cache_cmp_t, l, page_table, pos_row, wk, wv)
            q5 = q.reshape(dbatch, dseq, N_KV, HPG, HEAD_DIM)
            qp = jnp.einsum('btghd,gk->bhgtkd', q5, jnp.eye(N_KV, dtype=q.dtype)).reshape(
                dbatch, cols, KV_W).astype(bf16)
            oc_t, sel_mask = _nsa_sample_cmp(kc, vct, jnp.swapaxes(qp, 1, 2), bias_c_s, t_pos_row,
                                             n_sel_s, n_sel_pad)
            g4 = gates[:, :N_KV * NG_SLOT].reshape(dbatch, dseq, N_KV, NG_SLOT)[..., :3 * HPG]
            g4 = g4.reshape(dbatch, dseq, N_KV, HPG, 3)
            gates_c = jnp.transpose(g4, (0, 3, 2, 1, 4)).reshape(dbatch, cols, 3)
            gates_c = jnp.pad(gates_c, ((0, 0), (0, 0), (0, LANES - 3)))
            sel_pad = jnp.pad(sel_new, ((0, 0), (0, PAGE_SIZE - dseq), (0, 0)))
            win_all = jnp.concatenate([state_win_kv[l].reshape(dbatch, w_buf, PAIR_W), win_new], axis=1)
            win_pad = jnp.pad(win_all, ((0, 0), (0, n_win_pad - w_buf - dseq), (0, 0)))
            o_q = _nsa_sample_sel(cache_sel_t, l, page_table, sel_pad, win_pad, qp, sel_mask, oc_t,
                                  gates_c, bias_s_s, bias_new_s, bias_w_s, t_pos_col)
            o6 = o_q.reshape(dbatch, HPG, N_KV, dseq, N_KV, HEAD_DIM)
            o = jnp.einsum('bhgtgd->btghd', o6).reshape(m_s, N_HEADS * HEAD_DIM)
            r, s_new = _retention(rqk, rv, gts, gn_w, gn_b, state_ret[l], *dec_s, dbatch, dseq, dseq, f32)
            kv6 = lambda a: a.reshape(dbatch, -1, 2, N_KV, HEAD_DIM)
            return o, r, (kv6(cmp_new), kv6(sel_new), kv6(win_all[:, -w_buf:]), s_new)

        rope_s = (cos_s, sin_s, pl.BlockSpec((tm_s, RET_DK // 2), lambda i, j: (i, 0)))
        xs, st = _layer(xs, mod_s, lw, dseq, tm_s, f32, HEAD_DIM ** -0.5, False, rope_s, attend_sample)
        st_s.append(st)

    y_p = _norm("norm_final", xp, norm_final, None, None, seq, f32).reshape(batch, seq, D_MODEL)
    y_s = _norm("norm_final", xs, norm_final, None, None, dseq, f32).reshape(dbatch, dseq, D_MODEL)
    cmp_p, sel_p, win_p, ret_p = [jnp.stack([st[i] for st in st_p]) for i in range(4)]
    cmp_s, sel_s, win_s, ret_s = [jnp.stack([st[i] for st in st_s]) for i in range(4)]
    return (y_p, y_s, cmp_p, sel_p, win_p, ret_p, cmp_s, sel_s, win_s, ret_s)
```

```python
import functools
import math

import jax
import jax.numpy as jnp
from jax import lax
from jax.experimental import pallas as pl
from jax.experimental.pallas import tpu as pltpu

D_MODEL = 2048
DEPTH = 4
PAST_LEN = 16384
PAGE_SIZE = 128
N_HEADS = 16
HEAD_DIM = 64
N_KV = 4
HPG = N_HEADS // N_KV
CMP_BLOCK = 32
SEL_BLOCK = 64
TOP_K = 16
WINDOW = 512
RET_HEADS = 4
RET_DK = 256
RET_DV = 512
RET_CHUNK = 128
ROPE_BASE = 10000.0
D_FF = 4 * D_MODEL
N_BUCKETS = 32
REL_MAX_DIST = 2048
NORM_EPS = 1e-6
FORCE_BONUS = 1e6
NEG = -1e30

KV_W = N_KV * HEAD_DIM
PAIR_W = 2 * KV_W
NG_SLOT = 16
NG_W = 128

LANES = 128
SUBLANES = 8
VMEM_LIMIT_BYTES = 56 * 1024 * 1024

QB = 128
KB = 128
N_BIAS_TILES = 14
SEL_SUB = 8
LINEAR_TN = 1024
LINEAR_TK = 2048
LOG2E = math.log2(math.e)

f32 = jnp.float32
bf16 = jnp.bfloat16


def _cparams(sem):
    return pltpu.CompilerParams(dimension_semantics=sem, vmem_limit_bytes=VMEM_LIMIT_BYTES)


def _linear(name, xs, ws, extras, extra_specs, epilogue, out_shapes, out_specs, tm, tn, tk=None, n=None, w_col0=0):
    m = xs[0].shape[0]
    n = ws[0].shape[1] if n is None else n
    nx, ne = len(xs), len(extras)
    if tk is None:
        def body(*refs):
            x_refs, w_refs = refs[:nx], refs[nx:2 * nx]
            e_refs = refs[2 * nx:2 * nx + ne]
            o_refs = refs[2 * nx + ne:]
            accs = [jnp.dot(x[...].astype(bf16), w[...], preferred_element_type=f32)
                    for x, w in zip(x_refs, w_refs)]
            epilogue(accs, e_refs, o_refs)

        in_specs = ([pl.BlockSpec((tm, x.shape[1]), lambda i, j: (i, 0)) for x in xs]
                    + [pl.BlockSpec((w.shape[0], tn), lambda i, j: (0, j + w_col0)) for w in ws]
                    + list(extra_specs))
        return pl.pallas_call(
            body, grid=(m // tm, n // tn), in_specs=in_specs, out_specs=out_specs,
            out_shape=out_shapes, compiler_params=_cparams(("parallel", "arbitrary")),
            name=name)(*xs, *ws, *extras)

    assert nx == 1
    nk = xs[0].shape[1] // tk

    def body_k(x_ref, w_ref, *rest):
        e_refs, o_refs, acc_ref = rest[:ne], rest[ne:-1], rest[-1]
        k = pl.program_id(2)
        part = jnp.dot(x_ref[...].astype(bf16), w_ref[...], preferred_element_type=f32)

        @pl.when(k == 0)
        def _():
            acc_ref[...] = part

        @pl.when(k > 0)
        def _():
            acc_ref[...] += part

        @pl.when(k == nk - 1)
        def _():
            epilogue([acc_ref[...]], e_refs, o_refs)

    ij = lambda spec: pl.BlockSpec(spec.block_shape, lambda i, j, k, f=spec.index_map: f(i, j))
    single = not isinstance(out_specs, (list, tuple))
    outs = ij(out_specs) if single else [ij(s) for s in out_specs]
    in_specs = ([pl.BlockSpec((tm, tk), lambda i, j, k: (i, k)), pl.BlockSpec((tk, tn), lambda i, j, k: (k, j))]
                + [ij(s) for s in extra_specs])
    return pl.pallas_call(
        body_k, grid=(m // tm, n // tn, nk), in_specs=in_specs, out_specs=outs, out_shape=out_shapes,
        scratch_shapes=[pltpu.VMEM((tm, tn), f32)],
        compiler_params=_cparams(("parallel", "arbitrary", "arbitrary")), name=name)(*xs, *ws, *extras)


def _tile_spec(tm, tn, col0=0):
    return pl.BlockSpec((tm, tn), lambda i, j: (i, j + col0))


def _mod_spec(arr, tm, tn, rows_per_batch):
    if arr.ndim == 3:
        return pl.BlockSpec((None, 1, tn), lambda i, j: ((i * tm) // rows_per_batch, 0, j))
    return pl.BlockSpec((tm, tn), lambda i, j: (i, j))


def _mod_all(c_all, w_mod, b_mod):
    depth, d, n = w_mod.shape
    tn = 1024

    def body(c_ref, w_ref, b_ref, o_ref):
        c = c_ref[...]
        a = (c * jax.nn.sigmoid(c)).astype(bf16)
        o_ref[...] = jnp.dot(a, w_ref[...].astype(bf16), preferred_element_type=f32) + b_ref[...]

    return pl.pallas_call(
        body, grid=(depth, n // tn),
        in_specs=[pl.BlockSpec(c_all.shape, lambda l, j: (0, 0)),
                  pl.BlockSpec((None, d, tn), lambda l, j: (l, 0, j)),
                  pl.BlockSpec((None, 1, tn), lambda l, j: (l, 0, j))],
        out_specs=pl.BlockSpec((None, c_all.shape[0], tn), lambda l, j: (l, 0, j)),
        out_shape=jax.ShapeDtypeStruct((depth, c_all.shape[0], n), f32),
        compiler_params=_cparams(("parallel", "parallel")), name="mod_all",
    )(c_all, w_mod, b_mod.reshape(depth, 1, n))


def _norm(name, x, gain, sc, sh, rows_per_batch, out_dtype):
    m, d = x.shape
    tm = min(256, m)
    has_mod = sc is not None

    def body(*refs):
        if has_mod:
            x_ref, g_ref, sc_ref, sh_ref, o_ref = refs
        else:
            x_ref, g_ref, o_ref = refs
        xf = x_ref[...]
        y = xf * lax.rsqrt(jnp.mean(xf * xf, axis=-1, keepdims=True) + NORM_EPS) * g_ref[...]
        if has_mod:
            y = y * (1.0 + sc_ref[...]) + sh_ref[...]
        o_ref[...] = y.astype(out_dtype)

    in_specs = [pl.BlockSpec((tm, d), lambda i: (i, 0)), pl.BlockSpec((1, d), lambda i: (0, 0))]
    args = [x, gain.reshape(1, d)]
    if has_mod:
        for a in (sc, sh):
            if a.ndim == 3:
                in_specs.append(pl.BlockSpec((None, 1, d), lambda i: ((i * tm) // rows_per_batch, 0, 0)))
            else:
                in_specs.append(pl.BlockSpec((tm, d), lambda i: (i, 0)))
            args.append(a)
    return pl.pallas_call(
        body, grid=(m // tm,), in_specs=in_specs,
        out_specs=pl.BlockSpec((tm, d), lambda i: (i, 0)),
        out_shape=jax.ShapeDtypeStruct((m, d), out_dtype),
        compiler_params=_cparams(("parallel",)), name=name)(*args)


def _t5_bucket(dist):
    n = jnp.maximum(dist, 0)
    exact = N_BUCKETS // 2
    nf = jnp.maximum(n, exact).astype(f32)
    large = exact + (jnp.log(nf / exact) / math.log(REL_MAX_DIST / exact) * (N_BUCKETS - exact)).astype(jnp.int32)
    large = jnp.minimum(large, N_BUCKETS - 1)
    return jnp.where(n < exact, n, large)


def _bias_from_dist(dist, rel_bias, scale=1.0):
    r, c = dist.shape
    tr = next(t for t in (512, 256, 128, 64, 32, 16, 8) if r % t == 0)

    def body(tab_ref, d_ref, o_ref):
        h = pl.program_id(0)
        bucket = _t5_bucket(d_ref[...])
        acc = jnp.zeros(bucket.shape, f32)
        for k in range(N_BUCKETS):
            acc = jnp.where(bucket == k, tab_ref[k, h], acc)
        o_ref[...] = acc * scale

    return pl.pallas_call(
        body, grid=(N_HEADS, r // tr),
        in_specs=[pl.BlockSpec(memory_space=pltpu.SMEM),
                  pl.BlockSpec((tr, c), lambda h, i: (i, 0))],
        out_specs=pl.BlockSpec((None, tr, c), lambda h, i: (h, i, 0)),
        out_shape=jax.ShapeDtypeStruct((N_HEADS, r, c), f32),
        compiler_params=_cparams(("parallel", "parallel")), name="t5_bias",
    )(rel_bias, dist)


def _compress_accumulate(load_rows, pos_ref, wk_ref, wv_ref, rows):
    acc_k = jnp.zeros((rows, KV_W), f32)
    acc_v = jnp.zeros((rows, KV_W), f32)
    for c in range(CMP_BLOCK):
        lo = c * PAIR_W
        xk = (load_rows(c, 0) + pos_ref[:, lo:lo + KV_W]).astype(bf16)
        xv = (load_rows(c, 1) + pos_ref[:, lo + KV_W:lo + PAIR_W]).astype(bf16)
        acc_k = acc_k + jnp.dot(xk, wk_ref[c], preferred_element_type=f32)
        acc_v = acc_v + jnp.dot(xv, wv_ref[c], preferred_element_type=f32)
    return acc_k, acc_v


def _compress_prompt(kv_raw, pos_row, wk, wv, batch):
    seq = kv_raw.shape[1] // batch
    n = seq // CMP_BLOCK
    n_lane_tiles = PAIR_W // LANES

    def body(*refs):
        x_refs = refs[:n_lane_tiles]
        pos_ref, wk_ref, wv_ref, kc_ref, vct_ref = refs[n_lane_tiles:]

        def load_rows(c, kv):
            tiles = [x_refs[kv * (KV_W // LANES) + j][pl.ds(c, n, stride=CMP_BLOCK), :]
                     for j in range(KV_W // LANES)]
            return jnp.concatenate(tiles, axis=1)
        acc_k, acc_v = _compress_accumulate(load_rows, pos_ref, wk_ref, wv_ref, n)
        kc_ref[...] = acc_k.astype(bf16)
        vct_ref[...] = acc_v.T.astype(bf16)

    lane_tile = lambda j: pl.BlockSpec((None, seq, LANES), lambda b: (0, b, j))
    return pl.pallas_call(
        body, grid=(batch,),
        in_specs=[lane_tile(j) for j in range(n_lane_tiles)]
        + [pl.BlockSpec(pos_row.shape, lambda b: (0, 0)),
           pl.BlockSpec(wk.shape, lambda b: (0, 0, 0)),
           pl.BlockSpec(wv.shape, lambda b: (0, 0, 0))],
        out_specs=[pl.BlockSpec((None, n, KV_W), lambda b: (b, 0, 0)),
                   pl.BlockSpec((None, KV_W, n), lambda b: (b, 0, 0))],
        out_shape=[jax.ShapeDtypeStruct((batch, n, KV_W), bf16),
                   jax.ShapeDtypeStruct((batch, KV_W, n), bf16)],
        compiler_params=_cparams(("parallel",)), name="compress_prompt",
    )(*([kv_raw] * n_lane_tiles), pos_row, wk, wv)


CMP_PAGES_PER_STEP = 32


def _gather_pages(pt_ref, cache_ref, xbuf, sems, layer, pp, steps, n_steps):
    g = pl.program_id(0) * steps + pl.program_id(1)
    slot = g % 2

    def copies(step, into):
        row, first = step // steps, (step % steps) * pp
        return [pltpu.make_async_copy(cache_ref.at[layer, pt_ref[row, first + p]], xbuf.at[into, p],
                                      sems.at[into, p]) for p in range(pp)]

    @pl.when(g == 0)
    def _():
        for c in copies(g, slot):
            c.start()

    @pl.when(g + 1 < n_steps)
    def _():
        for c in copies(g + 1, 1 - slot):
            c.start()

    for c in copies(g, slot):
        c.wait()
    return slot


CMP_ROW_PITCH = CMP_BLOCK + SUBLANES


def _compress_sample(cache_t, layer, page_table, pos_row, wk, wv):
    batch, n_pages = page_table.shape
    blocks_per_page = PAGE_SIZE // CMP_BLOCK
    pp = CMP_PAGES_PER_STEP
    steps = n_pages // pp
    rows = pp * blocks_per_page
    n_lane_tiles = PAIR_W // LANES
    page_pitch = blocks_per_page * CMP_ROW_PITCH

    def body(pt_ref, cache_ref, pos_ref, wk_ref, wv_ref, kc_ref, vct_ref, xbuf, sems, tscr):
        slot = _gather_pages(pt_ref, cache_ref, xbuf, sems, layer, pp, steps, batch * steps)

        def untranspose(p, carry):
            base = pl.multiple_of(p * page_pitch, SUBLANES)
            for j in range(n_lane_tiles):
                tile = xbuf[slot, p, j * LANES:(j + 1) * LANES, :].T
                for n in range(blocks_per_page):
                    tscr[j, pl.ds(base + n * CMP_ROW_PITCH, CMP_BLOCK), :] = tile[n * CMP_BLOCK:(n + 1) * CMP_BLOCK]
            return carry
        lax.fori_loop(0, pp, untranspose, 0)

        def load_rows(c, kv):
            tiles = [tscr[kv * (KV_W // LANES) + j, pl.ds(c, rows, stride=CMP_ROW_PITCH), :]
                     for j in range(KV_W // LANES)]
            return jnp.concatenate(tiles, axis=1)
        acc_k, acc_v = _compress_accumulate(load_rows, pos_ref, wk_ref, wv_ref, rows)
        kc_ref[...] = acc_k.astype(bf16)
        vct_ref[...] = acc_v.T.astype(bf16)

    grid_spec = pltpu.PrefetchScalarGridSpec(
        num_scalar_prefetch=1, grid=(batch, steps),
        in_specs=[pl.BlockSpec(memory_space=pl.ANY),
                  pl.BlockSpec(pos_row.shape, lambda b, s, pt: (0, 0)),
                  pl.BlockSpec(wk.shape, lambda b, s, pt: (0, 0, 0)),
                  pl.BlockSpec(wv.shape, lambda b, s, pt: (0, 0, 0))],
        out_specs=[pl.BlockSpec((None, rows, KV_W), lambda b, s, pt: (b, s, 0)),
                   pl.BlockSpec((None, KV_W, rows), lambda b, s, pt: (b, 0, s))],
        scratch_shapes=[pltpu.VMEM((2, pp, PAIR_W, PAGE_SIZE), f32), pltpu.SemaphoreType.DMA((2, pp)),
                        pltpu.VMEM((n_lane_tiles, pp * page_pitch, LANES), f32)])
    n_cmp = n_pages * blocks_per_page
    return pl.pallas_call(
        body, grid_spec=grid_spec,
        out_shape=[jax.ShapeDtypeStruct((batch, n_cmp, KV_W), bf16),
                   jax.ShapeDtypeStruct((batch, KV_W, n_cmp), bf16)],
        compiler_params=_cparams(("arbitrary", "arbitrary")), name="compress_sample",
    )(page_table, cache_t, pos_row, wk, wv)


def _softmax_cols(s, mask, exp=jnp.exp):
    sm = jnp.where(mask, s, NEG)
    m = jnp.max(sm, axis=0, keepdims=True)
    e = exp(sm - m)
    p = e / jnp.sum(e, axis=0, keepdims=True)
    return jnp.where(mask, p, 0.0)


def _topk_mask(score, score_scr, n_rows, k, groups):
    n_pad, c = score.shape
    per = -(-n_rows // groups)
    assert groups * per <= n_pad
    rows = lax.broadcasted_iota(jnp.int32, score.shape, 0)
    lane_grp = lax.broadcasted_iota(jnp.int32, (1, c), 1) // (c // groups)
    score_scr[...] = score

    def step(j, cnt):
        sj = score_scr[pl.ds(j, 1), :]
        for h in range(1, groups):
            sj = jnp.where(lane_grp == h, score_scr[pl.ds(j + h * per, 1), :], sj)
        jv = j + per * lane_grp
        ahead = (sj > score) | ((sj == score) & (jv < rows))
        return cnt + jnp.where(ahead, 1.0, 0.0)

    cnt = lax.fori_loop(0, per, step, jnp.zeros(score.shape, f32))
    total = cnt
    for h in range(1, groups):
        total = total + pltpu.roll(cnt, h * (c // groups), axis=1)
    return jnp.where(total < k, 1.0, 0.0)


def _topk_mask_unrolled(score, score_scr, k):
    n_rows, c = score.shape
    score_scr[...] = score
    n_tiles = n_rows // SUBLANES
    tiles = [score[SUBLANES * v:SUBLANES * (v + 1)] for v in range(n_tiles)]
    row_in_tile = lax.broadcasted_iota(jnp.int32, (SUBLANES, c), 0)
    cnt = [jnp.zeros((SUBLANES, c), f32) for _ in range(n_tiles)]
    for j in range(n_rows):
        sj = score_scr[j:j + 1, :]
        vj = j // SUBLANES
        for v in range(n_tiles):
            if v > vj:
                ahead = sj >= tiles[v]
            elif v < vj:
                ahead = sj > tiles[v]
            else:
                ahead = (sj > tiles[v]) | ((sj == tiles[v]) & (row_in_tile > j % SUBLANES))
            cnt[v] = cnt[v] + jnp.where(ahead, 1.0, 0.0)
    return jnp.concatenate([jnp.where(cv < k, 1.0, 0.0) for cv in cnt], axis=0)


def _nsa_prompt(q, kc, vct, k_rows, vt_rows, gates, bias_c, bias_t, batch, seq):
    nq = seq // QB
    n_cmp = seq // CMP_BLOCK
    n_sel = seq // SEL_BLOCK
    ratio = SEL_BLOCK // CMP_BLOCK
    k_eff = min(TOP_K, n_sel)
    wq = HPG * QB
    win_chunks = WINDOW // KB

    def body(q_ref, kc_ref, vct_ref, ks_ref, vst_ref, kw_ref, vwt_ref, g_ref, bc_ref, bt_ref,
             o_ref, imp_scr, score_scr, sel_scr, gate_scr):
        g = pl.program_id(1)
        i = pl.program_id(2)
        q0 = i * QB

        qt = q_ref[...].astype(f32).T
        qcat = jnp.concatenate([qt[h * HEAD_DIM:(h + 1) * HEAD_DIM] for h in range(HPG)], axis=1)
        qrep = jnp.concatenate([qcat] * N_KV, axis=0)
        row_grp = lax.broadcasted_iota(jnp.int32, (KV_W, wq), 0) // HEAD_DIM
        qpad = jnp.where(row_grp == g, qrep, 0.0).astype(bf16)

        t_row = q0 + lax.broadcasted_iota(jnp.int32, (1, wq), 1) % QB

        s_c = jnp.dot(kc_ref[...], qpad, preferred_element_type=f32) + bc_ref[...]
        c_end = lax.broadcasted_iota(jnp.int32, (n_cmp, wq), 0) * CMP_BLOCK + (CMP_BLOCK - 1)
        p_c = _softmax_cols(s_c, c_end <= t_row, jnp.exp2)
        o_c = jnp.dot(vct_ref[...], p_c.astype(bf16), preferred_element_type=f32)

        imp_scr[...] = sum(p_c[:, h * QB:(h + 1) * QB] for h in range(HPG))
        imp = sum(imp_scr[pl.ds(r, n_sel, stride=ratio), :] for r in range(ratio))
        blk = lax.broadcasted_iota(jnp.int32, (n_sel, QB), 0)
        t_q = q0 + lax.broadcasted_iota(jnp.int32, (n_sel, QB), 1)
        cur = t_q // SEL_BLOCK
        forced = (blk == 0) | (blk == cur) | (blk == cur - 1)
        score = imp + jnp.where(forced, FORCE_BONUS, 0.0)
        score = jnp.where(blk * SEL_BLOCK <= t_q, score, NEG)
        sel_scr[...] = _topk_mask_unrolled(score, score_scr, k_eff)

        kj = lax.broadcasted_iota(jnp.int32, (KB, QB), 0)
        qi = lax.broadcasted_iota(jnp.int32, (KB, QB), 1)
        rel = qi - kj
        upper = kj < SEL_BLOCK

        def scores(k_ref, chunk, bias, mask):
            k0 = pl.multiple_of(chunk * KB, KB)
            s = jnp.dot(k_ref[pl.ds(k0, KB), :], qpad, preferred_element_type=f32)
            penalty = jnp.where(mask, 0.0, NEG)
            return s + bias + jnp.concatenate([penalty] * HPG, axis=1)

        def weighted_values(vt_ref, chunks, ps):
            return sum(jnp.dot(vt_ref[:, pl.ds(pl.multiple_of(c * KB, KB), KB)], p.astype(bf16),
                               preferred_element_type=f32) for c, p in zip(chunks, ps))

        def col_max(ss):
            return functools.reduce(jnp.maximum, [jnp.max(s, axis=0, keepdims=True) for s in ss])

        def sel_chunks(j):
            return [jnp.minimum(j * SEL_SUB + r, i) for r in range(SEL_SUB)]

        def sel_scores(j):
            ss = []
            for r, cc in enumerate(sel_chunks(j)):
                rows2 = sel_scr[pl.ds(pl.multiple_of(cc * (KB // SEL_BLOCK), KB // SEL_BLOCK),
                                      KB // SEL_BLOCK), :]
                chosen = jnp.where(upper, rows2[0:1, :], rows2[1:2, :]) > 0.5
                mask = chosen & (rel + (i - (j * SEL_SUB + r)) * KB >= 0)
                ss.append(scores(ks_ref, cc, bt_ref[jnp.minimum(i - cc, N_BIAS_TILES - 1)], mask))
            return ss

        def sel_step(j, carry):
            m, l, acc = carry
            ss = sel_scores(j)
            m_new = jnp.maximum(m, col_max(ss))
            alpha = jnp.exp2(m - m_new)
            ps = [jnp.exp2(s - m_new) for s in ss]
            l = alpha * l + sum(jnp.sum(p, axis=0, keepdims=True) for p in ps)
            acc = alpha * acc + weighted_values(vst_ref, sel_chunks(j), ps)
            return m_new, l, acc

        init = (jnp.full((1, wq), NEG, f32), jnp.zeros((1, wq), f32), jnp.zeros((HEAD_DIM, wq), f32))
        _, l_s, acc_s = lax.fori_loop(0, (i + SEL_SUB) // SEL_SUB, sel_step, init)
        o_s = acc_s / l_s

        chunks, ss = [], []
        for dd in range(win_chunks + 1):
            c = i - dd
            dist = rel + (dd * KB + jnp.where(c >= 0, 0, 2 * WINDOW))
            mask = (dist >= 0) & (dist < WINDOW)
            chunks.append(jnp.maximum(c, 0))
            ss.append(scores(kw_ref, chunks[-1], bt_ref[dd], mask))
        m_w = col_max(ss)
        ps = [jnp.exp2(s - m_w) for s in ss]
        l_w = sum(jnp.sum(p, axis=0, keepdims=True) for p in ps)
        o_w = weighted_values(vwt_ref, chunks, ps) / l_w

        gate_scr[...] = g_ref[...].T
        gg = gate_scr[pl.ds(pl.multiple_of(g * NG_SLOT, NG_SLOT), NG_SLOT), :]
        outs = []
        for h in range(HPG):
            sl = slice(h * QB, (h + 1) * QB)
            outs.append(gg[3 * h:3 * h + 1] * o_c[:, sl] + gg[3 * h + 1:3 * h + 2] * o_s[:, sl]
                        + gg[3 * h + 2:3 * h + 3] * o_w[:, sl])
        o_ref[...] = jnp.concatenate(outs, axis=0).T.astype(o_ref.dtype)

    return pl.pallas_call(
        body, grid=(batch, N_KV, nq),
        in_specs=[
            pl.BlockSpec((QB, KV_W), lambda b, g, i: (b * nq + i, g)),
            pl.BlockSpec((None, n_cmp, KV_W), lambda b, g, i: (b, 0, 0)),
            pl.BlockSpec((None, HEAD_DIM, n_cmp), lambda b, g, i: (b, g, 0)),
            pl.BlockSpec((None, seq, KV_W), lambda b, g, i: (1, b, 0)),
            pl.BlockSpec((None, None, HEAD_DIM, seq), lambda b, g, i: (1, b, g, 0)),
            pl.BlockSpec((None, seq, KV_W), lambda b, g, i: (2, b, 0)),
            pl.BlockSpec((None, None, HEAD_DIM, seq), lambda b, g, i: (2, b, g, 0)),
            pl.BlockSpec((QB, NG_W), lambda b, g, i: (b * nq + i, 0)),
            pl.BlockSpec((None, None, n_cmp, wq), lambda b, g, i: (i, g, 0, 0)),
            pl.BlockSpec((N_BIAS_TILES, None, KB, wq), lambda b, g, i: (0, g, 0, 0)),
        ],
        out_specs=pl.BlockSpec((QB, KV_W), lambda b, g, i: (b * nq + i, g)),
        out_shape=jax.ShapeDtypeStruct((batch * seq, N_HEADS * HEAD_DIM), bf16),
        scratch_shapes=[pltpu.VMEM((n_cmp, QB), f32), pltpu.VMEM((n_sel, QB), f32),
                        pltpu.VMEM((n_sel, QB), f32), pltpu.VMEM((NG_W, QB), f32)],
        compiler_params=_cparams(("parallel", "parallel", "arbitrary")), name="nsa_prompt",
    )(q, kc, vct, k_rows, vt_rows, k_rows, vt_rows, gates, bias_c, bias_t)


def _retention(qk, v, rg, gn_w, gn_b, state0, dec_in, dec_q, dec_k, dec_c, batch, length, chunk, out_dtype):
    n = length // chunk
    w_qk, w_v = RET_HEADS * RET_DK, RET_HEADS * RET_DV

    def body(q_ref, k_ref, v_ref, rg_ref, gw_ref, gb_ref, s0_ref, din_ref, dq_ref, dk_ref, dc_ref,
             r_ref, s_ref):
        @pl.when(pl.program_id(1) == 0)
        def _():
            s_ref[...] = s0_ref[...]

        for h in range(RET_HEADS):
            qs, vs = slice(h * RET_DK, (h + 1) * RET_DK), slice(h * RET_DV, (h + 1) * RET_DV)
            qc = q_ref[:, qs].astype(bf16)
            kf = k_ref[:, qs].astype(f32)
            vc = v_ref[:, vs].astype(bf16)
            st = s_ref[h]
            inner = lax.dot_general(qc, kf.astype(bf16), (((1,), (1,)), ((), ())),
                                    preferred_element_type=f32) * din_ref[h]
            o = jnp.dot(inner.astype(bf16), vc, preferred_element_type=f32)
            o = o + jnp.dot(qc, st.astype(bf16), preferred_element_type=f32) * dq_ref[h]
            kd = (kf * dk_ref[h]).astype(bf16)
            s_ref[h] = st * dc_ref[h] + lax.dot_general(kd, vc, (((0,), (0,)), ((), ())),
                                                        preferred_element_type=f32)
            mu = jnp.mean(o, axis=-1, keepdims=True)
            var = jnp.mean(jnp.square(o - mu), axis=-1, keepdims=True)
            y = (o - mu) * lax.rsqrt(var + NORM_EPS) * gw_ref[:, vs] + gb_ref[:, vs]
            gate = rg_ref[:, vs]
            r_ref[:, vs] = (gate * jax.nn.sigmoid(gate) * y).astype(r_ref.dtype)

    seq_spec = lambda w, col0=0: pl.BlockSpec((chunk, w), lambda b, c: (b * n + c, col0))
    full_spec = lambda a: pl.BlockSpec(a.shape, lambda b, c: (0,) * a.ndim)
    state_spec = pl.BlockSpec((None, RET_HEADS, RET_DK, RET_DV), lambda b, c: (b, 0, 0, 0))
    return pl.pallas_call(
        body, grid=(batch, n),
        in_specs=[seq_spec(w_qk), seq_spec(w_qk, 1), seq_spec(w_v), seq_spec(w_v),
                  full_spec(gn_w), full_spec(gn_b), state_spec,
                  full_spec(dec_in), full_spec(dec_q), full_spec(dec_k), full_spec(dec_c)],
        out_specs=[seq_spec(w_v), state_spec],
        out_shape=[jax.ShapeDtypeStruct((batch * length, w_v), out_dtype),
                   jax.ShapeDtypeStruct((batch, RET_HEADS, RET_DK, RET_DV), f32)],
        compiler_params=_cparams(("parallel", "arbitrary")), name="retention",
    )(qk, qk, v, rg, gn_w, gn_b, state0, dec_in, dec_q, dec_k, dec_c)


def _retention_decays(chunk):
    lg = jnp.log(1.0 - jnp.exp2(-5.0 - jnp.arange(RET_HEADS, dtype=f32)))
    i = jnp.arange(chunk, dtype=f32)
    diff = i[:, None] - i[None, :]
    dec_in = jnp.where(diff >= 0, jnp.exp(jnp.maximum(diff, 0.0)[None] * lg[:, None, None]), 0.0)
    dec_q = jnp.exp((i[None, :] + 1.0) * lg[:, None])[..., None]
    dec_k = jnp.exp((chunk - 1.0 - i)[None, :] * lg[:, None])[..., None]
    dec_c = jnp.exp(chunk * lg)[:, None, None]
    return dec_in, dec_q, dec_k, dec_c


def _rope_tables(pos):
    half = RET_DK // 2
    inv = ROPE_BASE ** (-jnp.arange(half, dtype=f32) / half)
    ang = pos.astype(f32)[:, None] * inv[None, :]
    return jnp.cos(ang), jnp.sin(ang)


def _nsa_sample_cmp(kc, vct, qpt, bias_c, t_pos_row, n_sel, n_sel_pad):
    batch, n_cmp, _ = kc.shape
    ratio = SEL_BLOCK // CMP_BLOCK
    cols = qpt.shape[2]
    grp_cols = cols // HPG

    def body(kc_ref, vct_ref, q_ref, bc_ref, t_ref, oc_ref, sel_ref, imp_scr, score_scr):
        t_row = t_ref[...]
        s_c = jnp.dot(kc_ref[...], q_ref[...], preferred_element_type=f32) + bc_ref[...]
        c_end = lax.broadcasted_iota(jnp.int32, (n_cmp, cols), 0) * CMP_BLOCK + (CMP_BLOCK - 1)
        p_c = _softmax_cols(s_c, c_end <= t_row)
        oc_ref[...] = jnp.dot(vct_ref[...], p_c.astype(bf16), preferred_element_type=f32)
        assert HPG == 4
        pair = p_c + pltpu.roll(p_c, 2 * grp_cols, axis=1)
        imp_rep = pair + pltpu.roll(pair, grp_cols, axis=1)
        imp_scr[...] = jnp.zeros(imp_scr.shape, f32)
        imp_scr[pl.ds(0, n_cmp), :] = imp_rep
        imp = sum(imp_scr[pl.ds(r, n_sel_pad, stride=ratio), :] for r in range(ratio))
        blk = lax.broadcasted_iota(jnp.int32, (n_sel_pad, cols), 0)
        cur = t_row // SEL_BLOCK
        forced = (blk == 0) | (blk == cur) | (blk == cur - 1)
        score = imp + jnp.where(forced, FORCE_BONUS, 0.0)
        score = jnp.where(blk * SEL_BLOCK <= t_row, score, NEG)
        score = jnp.where(blk < n_sel, score, -jnp.inf)
        sel_ref[...] = _topk_mask(score, score_scr, n_sel, min(TOP_K, n_sel), HPG)

    return pl.pallas_call(
        body, grid=(batch,),
        in_specs=[pl.BlockSpec((None, n_cmp, KV_W), lambda b: (b, 0, 0)),
                  pl.BlockSpec((None, KV_W, n_cmp), lambda b: (b, 0, 0)),
                  pl.BlockSpec((None, KV_W, cols), lambda b: (b, 0, 0)),
                  pl.BlockSpec((n_cmp, cols), lambda b: (0, 0)),
                  pl.BlockSpec((1, cols), lambda b: (0, 0))],
        out_specs=[pl.BlockSpec((None, KV_W, cols), lambda b: (b, 0, 0)),
                   pl.BlockSpec((None, n_sel_pad, cols), lambda b: (b, 0, 0))],
        out_shape=[jax.ShapeDtypeStruct((batch, KV_W, cols), f32),
                   jax.ShapeDtypeStruct((batch, n_sel_pad, cols), f32)],
        scratch_shapes=[pltpu.VMEM((n_sel_pad * ratio, cols), f32), pltpu.VMEM((n_sel_pad, cols), f32)],
        compiler_params=_cparams(("parallel",)), name="nsa_sample_cmp",
    )(kc, vct, qpt, bias_c, t_pos_row)


SEL_PAGES_PER_STEP = 16


def _softmax_rows(s, mask):
    sm = jnp.where(mask, s, NEG)
    m = jnp.max(sm, axis=1, keepdims=True)
    e = jnp.exp(sm - m)
    p = e / jnp.sum(e, axis=1, keepdims=True)
    return jnp.where(mask, p, 0.0)


def _nsa_sample_sel(cache_t, layer, page_table, sel_new, win_all, qp, sel_mask, oc_t, gates_c,
                    bias_s, bias_new, bias_w, t_pos_col):
    batch, n_pages = page_table.shape
    pp = SEL_PAGES_PER_STEP
    steps = n_pages // pp
    keys = pp * PAGE_SIZE
    nq = qp.shape[1]
    blocks_per_page = PAGE_SIZE // SEL_BLOCK
    nb = pp * blocks_per_page
    n_win_pad = win_all.shape[1]
    w_buf = min(WINDOW, PAST_LEN)
    nt = (((1,), (1,)), ((), ()))
    tn = (((0,), (0,)), ((), ()))

    def body(pt_ref, cache_ref, new_ref, win_ref, q_ref, selm_ref, oc_ref, g_ref, bs_ref, bn_ref, bw_ref,
             t_ref, o_ref, xbuf, sems, m_scr, l_scr, acc_scr):
        s = pl.program_id(1)
        slot = _gather_pages(pt_ref, cache_ref, xbuf, sems, layer, pp, steps, batch * steps)

        @pl.when(s == 0)
        def _():
            m_scr[...] = jnp.full(m_scr.shape, NEG, f32)
            l_scr[...] = jnp.zeros(l_scr.shape, f32)
            acc_scr[...] = jnp.zeros(acc_scr.shape, f32)

        q = q_ref[...]
        t_col = t_ref[...]

        def chosen_keys(block_rows, n_keys):
            n_blocks = block_rows.shape[0]
            key_blk = lax.broadcasted_iota(jnp.int32, (n_blocks, n_keys), 1) // SEL_BLOCK
            spread = jnp.where(key_blk == lax.broadcasted_iota(jnp.int32, (n_blocks, n_keys), 0), 1.0, 0.0)
            return lax.dot_general(block_rows, spread, tn, preferred_element_type=f32) > 0.5

        def online(s_qk, mask, weighted_values):
            s_m = jnp.where(mask, s_qk, NEG)
            m_old = m_scr[...]
            m_new = jnp.maximum(m_old, jnp.max(s_m, axis=1, keepdims=True))
            alpha = jnp.exp(m_old - m_new)
            p = jnp.where(mask, jnp.exp(s_m - m_new), 0.0)
            l_scr[...] = alpha * l_scr[...] + jnp.sum(p, axis=1, keepdims=True)
            acc_scr[...] = alpha * acc_scr[...] + weighted_values(p.astype(bf16))
            m_scr[...] = m_new

        s_qk = jnp.concatenate(
            [jnp.dot(q, xbuf[slot, p, :KV_W, :].astype(bf16), preferred_element_type=f32) for p in range(pp)],
            axis=1) + bs_ref[...]
        mask = chosen_keys(selm_ref[pl.ds(pl.multiple_of(s * nb, nb), nb), :], keys)
        online(s_qk, mask, lambda pb: sum(
            lax.dot_general(pb[:, p * PAGE_SIZE:(p + 1) * PAGE_SIZE], xbuf[slot, p, KV_W:, :].astype(bf16), nt,
                            preferred_element_type=f32) for p in range(pp)))

        @pl.when(s == steps - 1)
        def _():
            kv_n = new_ref[...]
            s_n = lax.dot_general(q, kv_n[:, :KV_W].astype(bf16), nt, preferred_element_type=f32) + bn_ref[...]
            pos_n = PAST_LEN + lax.broadcasted_iota(jnp.int32, (nq, PAGE_SIZE), 1)
            mask_n = chosen_keys(selm_ref[pl.ds(n_pages * blocks_per_page, SUBLANES), :], PAGE_SIZE) & (pos_n <= t_col)
            online(s_n, mask_n, lambda pb: jnp.dot(pb, kv_n[:, KV_W:].astype(bf16), preferred_element_type=f32))
            o_s = acc_scr[...] / l_scr[...]

            kv_w = win_ref[...]
            kpos = (PAST_LEN - w_buf) + lax.broadcasted_iota(jnp.int32, (nq, n_win_pad), 1)
            dist = t_col - kpos
            mask_w = (dist >= 0) & (dist < WINDOW) & (kpos >= 0)
            s_w = lax.dot_general(q, kv_w[:, :KV_W].astype(bf16), nt, preferred_element_type=f32) + bw_ref[...]
            p_w = _softmax_rows(s_w, mask_w)
            o_w = jnp.dot(p_w.astype(bf16), kv_w[:, KV_W:].astype(bf16), preferred_element_type=f32)
            gts = g_ref[...]
            o_ref[...] = gts[:, 0:1] * oc_ref[...].T + gts[:, 1:2] * o_s + gts[:, 2:3] * o_w

    grid_spec = pltpu.PrefetchScalarGridSpec(
        num_scalar_prefetch=1, grid=(batch, steps),
        in_specs=[pl.BlockSpec(memory_space=pl.ANY),
                  pl.BlockSpec((None, PAGE_SIZE, PAIR_W), lambda b, s, pt: (b, 0, 0)),
                  pl.BlockSpec((None, n_win_pad, PAIR_W), lambda b, s, pt: (b, 0, 0)),
                  pl.BlockSpec((None, nq, KV_W), lambda b, s, pt: (b, 0, 0)),
                  pl.BlockSpec((None, sel_mask.shape[1], nq), lambda b, s, pt: (b, 0, 0)),
                  pl.BlockSpec((None, KV_W, nq), lambda b, s, pt: (b, 0, 0)),
                  pl.BlockSpec((None, nq, LANES), lambda b, s, pt: (b, 0, 0)),
                  pl.BlockSpec((nq, keys), lambda b, s, pt: (0, s)),
                  pl.BlockSpec((nq, PAGE_SIZE), lambda b, s, pt: (0, 0)),
                  pl.BlockSpec((nq, n_win_pad), lambda b, s, pt: (0, 0)),
                  pl.BlockSpec((nq, 1), lambda b, s, pt: (0, 0))],
        out_specs=pl.BlockSpec((None, nq, KV_W), lambda b, s, pt: (b, 0, 0)),
        scratch_shapes=[pltpu.VMEM((2, pp, PAIR_W, PAGE_SIZE), f32), pltpu.SemaphoreType.DMA((2, pp)),
                        pltpu.VMEM((nq, 1), f32), pltpu.VMEM((nq, 1), f32),
                        pltpu.VMEM((nq, KV_W), f32)])
    return pl.pallas_call(
        body, grid_spec=grid_spec,
        out_shape=jax.ShapeDtypeStruct((batch, nq, KV_W), f32),
        compiler_params=_cparams(("arbitrary", "arbitrary")), name="nsa_sample_sel",
    )(page_table, cache_t, sel_new, win_all, qp, sel_mask, oc_t, gates_c, bias_s, bias_new, bias_w,
      t_pos_col)


_OFF_Q = 0
_OFF_KV = N_HEADS * HEAD_DIM
_OFF_NG = _OFF_KV + 6 * KV_W
_OFF_RQ = _OFF_NG + 3 * N_HEADS
_OFF_RV = _OFF_RQ + 2 * RET_HEADS * RET_DK
_OFF_RG = _OFF_RV + RET_HEADS * RET_DV
_D_IN = _OFF_RG + RET_HEADS * RET_DV + 2 * D_MODEL


def _prep_w_in(w_in_l):
    w_attn = w_in_l[:, :_OFF_NG].astype(bf16)
    ng = w_in_l[:, _OFF_NG:_OFF_RQ].astype(bf16).reshape(D_MODEL, N_KV, 3 * HPG)
    w_ng = jnp.pad(ng, ((0, 0), (0, 0), (0, NG_SLOT - 3 * HPG))).reshape(D_MODEL, N_KV * NG_SLOT)
    w_ng = jnp.pad(w_ng, ((0, 0), (0, NG_W - N_KV * NG_SLOT)))
    w_ret = w_in_l[:, _OFF_RQ:].astype(bf16)
    return w_attn, w_ng, w_ret


def _prep_w_phi(w_phi_l, cmp_pos_l):
    eye = jnp.eye(N_KV, dtype=f32)
    wk = jnp.einsum('gh,cde->cgdhe', eye, w_phi_l[0]).reshape(CMP_BLOCK, KV_W, KV_W).astype(bf16)
    wv = jnp.einsum('gh,cde->cgdhe', eye, w_phi_l[1]).reshape(CMP_BLOCK, KV_W, KV_W).astype(bf16)
    pos = jnp.broadcast_to(cmp_pos_l[:, :, None, :], (2, CMP_BLOCK, N_KV, HEAD_DIM))
    pos_row = jnp.transpose(pos, (1, 0, 2, 3)).reshape(1, CMP_BLOCK * PAIR_W)
    return wk, wv, pos_row


def _layer(x, mod, lw, rows_per_batch, tm, act, q_scale, kv_layouts, rope, attend):
    m = x.shape[0]
    sh1, sc1, g1, sh2, sc2, g2 = mod
    h = _norm("norm_mix", x, lw["norm_mix"], sc1, sh1, rows_per_batch, act)

    def plain(dtype, scale=1.0):
        def epi(accs, e_refs, o_refs):
            o_refs[0][...] = (accs[0] * scale).astype(dtype)
        return epi

    w_attn, w_ng, w_ret = lw["w_attn"], lw["w_ng"], lw["w_ret"]
    tn = LINEAR_TN
    q = _linear("proj_q", [h], [w_attn], [], [], plain(act, q_scale),
                jax.ShapeDtypeStruct((m, _OFF_KV), act), _tile_spec(tm, tn), tm, tn, n=_OFF_KV)
    kv_cols = dict(n=3 * PAIR_W, w_col0=_OFF_KV // PAIR_W)
    pair_spec = lambda w: pl.BlockSpec((None, tm, w), lambda i, j: (j, i, 0))
    if kv_layouts:
        n_b, per_b = m // rows_per_batch, rows_per_batch // tm
        t_spec = lambda w: pl.BlockSpec((None, None, w, tm), lambda i, j: (j, i // per_b, 0, i % per_b))

        def kv_epi(accs, e_refs, o_refs):
            a = accs[0]
            a_t = a.T
            o_refs[0][...] = a
            o_refs[1][...] = a[:, :KV_W].astype(bf16)
            o_refs[2][...] = a_t[KV_W:].astype(bf16)
            o_refs[3][...] = a_t
        kv_raw = _linear("proj_kv", [h], [w_attn], [], [], kv_epi,
                         [jax.ShapeDtypeStruct((3, m, PAIR_W), f32), jax.ShapeDtypeStruct((3, m, KV_W), bf16),
                          jax.ShapeDtypeStruct((3, n_b, KV_W, rows_per_batch), bf16),
                          jax.ShapeDtypeStruct((3, n_b, PAIR_W, rows_per_batch), f32)],
                         [pair_spec(PAIR_W), pair_spec(KV_W), t_spec(KV_W), t_spec(PAIR_W)], tm, PAIR_W,
                         **kv_cols)
    else:
        kv_raw = _linear("proj_kv", [h], [w_attn], [], [], plain(f32),
                         jax.ShapeDtypeStruct((3, m, PAIR_W), f32), pair_spec(PAIR_W), tm, PAIR_W, **kv_cols)

    def sig_epi(accs, e_refs, o_refs):
        o_refs[0][...] = jax.nn.sigmoid(accs[0])
    gates = _linear("proj_ng", [h], [w_ng], [], [], sig_epi,
                    jax.ShapeDtypeStruct((m, NG_W), f32), _tile_spec(tm, NG_W), tm, NG_W)

    cos, sin, rope_spec = rope
    n_qk_tiles = RET_HEADS * RET_DK // tn

    def rot_epi(accs, e_refs, o_refs):
        a = accs[0]
        c, s = e_refs[0][...], e_refs[1][...]
        scale = jnp.where(pl.program_id(1) >= n_qk_tiles, RET_DK ** -0.5, 1.0)
        half = RET_DK // 2
        outs = []
        for hh in range(tn // RET_DK):
            x1 = a[:, hh * RET_DK:hh * RET_DK + half]
            x2 = a[:, hh * RET_DK + half:(hh + 1) * RET_DK]
            outs += [x1 * c - x2 * s, x2 * c + x1 * s]
        o_refs[0][...] = (jnp.concatenate(outs, axis=1) * scale).astype(act)

    n_rot = 2 * RET_HEADS * RET_DK
    rqk = _linear("proj_rqk", [h], [w_ret], [cos, sin], [rope_spec, rope_spec], rot_epi,
                  jax.ShapeDtypeStruct((m, n_rot), act), _tile_spec(tm, tn), tm, tn, n=n_rot)
    n_rv = RET_HEADS * RET_DV
    rv = _linear("proj_rv", [h], [w_ret], [], [], plain(act),
                 jax.ShapeDtypeStruct((m, n_rv), act), _tile_spec(tm, tn), tm, tn, n=n_rv, w_col0=n_rot // tn)
    gts = _linear("proj_gates", [h], [w_ret], [], [], plain(f32),
                  jax.ShapeDtypeStruct((m, 3 * D_MODEL), f32), _tile_spec(tm, tn), tm, tn,
                  n=3 * D_MODEL, w_col0=(n_rot + n_rv) // tn)

    o_nsa, r, states = attend(q, kv_raw, gates, rqk, rv, gts)

    def merge_epi(accs, e_refs, o_refs):
        o_refs[0][...] = (jax.nn.sigmoid(e_refs[0][...]) * accs[0]
                          + jax.nn.sigmoid(e_refs[1][...]) * accs[1]).astype(act)
    nt = D_MODEL // tn
    merged = _linear("merge", [o_nsa, r], [lw["w_nsa_proj"], lw["w_ret_proj"]], [gts, gts],
                     [_tile_spec(tm, tn, nt), _tile_spec(tm, tn, 2 * nt)], merge_epi,
                     jax.ShapeDtypeStruct((m, D_MODEL), act), _tile_spec(tm, tn), tm, tn)

    def resid_epi(accs, e_refs, o_refs):
        o_refs[0][...] = e_refs[0][...] + e_refs[1][...] * accs[0]
    x = _linear("out_proj", [merged], [lw["w_out"]], [x, g1],
                [_tile_spec(tm, tn), _mod_spec(g1, tm, tn, rows_per_batch)], resid_epi,
                jax.ShapeDtypeStruct((m, D_MODEL), f32), _tile_spec(tm, tn), tm, tn)

    h2 = _norm("norm_mlp", x, lw["norm_mlp"], sc2, sh2, rows_per_batch, act)

    def up_epi(accs, e_refs, o_refs):
        o_refs[0][...] = jnp.square(jnp.maximum(accs[0], 0.0)).astype(act)
    u = _linear("mlp_up", [h2], [lw["w_up"]], [], [], up_epi,
                jax.ShapeDtypeStruct((m, D_FF), act), _tile_spec(tm, tn), tm, tn)
    x = _linear("mlp_down", [u], [lw["w_down"]], [x, g2],
                [_tile_spec(tm, tn), _mod_spec(g2, tm, tn, rows_per_batch)], resid_epi,
                jax.ShapeDtypeStruct((m, D_MODEL), f32), _tile_spec(tm, tn), tm, tn, tk=LINEAR_TK)
    return x, states


def kernel(x_prompt, x_sample, c_prompt, c_sample, cache_cmp_kv, cache_sel_kv, state_win_kv, state_ret,
           page_table, rel_bias, w_mod, b_mod, norm_mix, norm_mlp, w_in, cmp_pos, w_phi, w_nsa_proj,
           ret_gn_w, ret_gn_b, w_ret_proj, w_out, w_up, w_down, norm_final):
    batch, seq, _ = x_prompt.shape
    dbatch, dseq, _ = x_sample.shape
    depth = w_in.shape[0]
    n_pool = cache_cmp_kv.shape[1]
    n_pages = page_table.shape[1]
    i32 = jnp.int32

    n_c = batch + dbatch
    c_all = jnp.pad(jnp.concatenate([c_prompt, c_sample], axis=0), ((0, (-n_c) % SUBLANES), (0, 0)))
    mod_all = _mod_all(c_all, w_mod, b_mod)

    nq = seq // QB
    n_cmp_p = seq // CMP_BLOCK
    kj = jnp.arange(KB, dtype=i32)[:, None]
    qi = jnp.arange(QB, dtype=i32)[None, :]
    dist_tiles = (jnp.arange(N_BIAS_TILES, dtype=i32)[:, None, None] * KB + qi[None] - kj[None])
    bias_t = _bias_from_dist(dist_tiles.reshape(N_BIAS_TILES * KB, QB), rel_bias, LOG2E)
    bias_t = bias_t.reshape(N_KV, HPG, N_BIAS_TILES, KB, QB).transpose(2, 0, 3, 1, 4).reshape(
        N_BIAS_TILES, N_KV, KB, HPG * QB)
    t_all = jnp.arange(seq, dtype=i32)
    c_end_p = jnp.arange(n_cmp_p, dtype=i32) * CMP_BLOCK + (CMP_BLOCK - 1)
    bias_c = _bias_from_dist(t_all[None, :] - c_end_p[:, None], rel_bias, LOG2E)
    bias_c = bias_c.reshape(N_KV, HPG, n_cmp_p, nq, QB).transpose(3, 0, 2, 1, 4).reshape(
        nq, N_KV, n_cmp_p, HPG * QB)

    cos_p, sin_p = _rope_tables(jnp.arange(seq, dtype=i32))
    dec_p = _retention_decays(RET_CHUNK)

    cols = HPG * N_KV * dseq
    t_cols = PAST_LEN + jnp.tile(jnp.arange(dseq, dtype=i32), HPG * N_KV)
    head_cols = jnp.tile(jnp.repeat(jnp.arange(N_KV, dtype=i32), dseq), HPG) * HPG + jnp.repeat(
        jnp.arange(HPG, dtype=i32), N_KV * dseq)
    t_new = PAST_LEN + jnp.arange(dseq, dtype=i32)

    def col_bias(key_pos):
        per_head = _bias_from_dist(t_new[:, None] - key_pos[None, :], rel_bias)
        t_idx = jnp.tile(jnp.arange(dseq, dtype=i32), HPG * N_KV)
        return per_head[head_cols, t_idx, :]

    n_cmp_s = PAST_LEN // CMP_BLOCK
    bias_c_s = col_bias(jnp.arange(n_cmp_s, dtype=i32) * CMP_BLOCK + (CMP_BLOCK - 1)).T
    bias_s_s = col_bias(jnp.arange(PAST_LEN, dtype=i32))
    bias_new_s = col_bias(PAST_LEN + jnp.arange(PAGE_SIZE, dtype=i32))
    w_buf = state_win_kv.shape[2]
    n_win_pad = -(-(w_buf + dseq) // LANES) * LANES
    bias_w_s = col_bias(PAST_LEN - w_buf + jnp.arange(n_win_pad, dtype=i32))
    n_sel_s = -(-(PAST_LEN + dseq) // SEL_BLOCK)
    n_sel_pad = -(-(n_pages * (PAGE_SIZE // SEL_BLOCK) + PAGE_SIZE // SEL_BLOCK) // SUBLANES) * SUBLANES
    assert n_sel_pad >= n_sel_s
    cos_s, sin_s = _rope_tables(t_new)
    cos_s, sin_s = jnp.tile(cos_s, (dbatch, 1)), jnp.tile(sin_s, (dbatch, 1))
    dec_s = _retention_decays(dseq)
    t_pos_row = t_cols.reshape(1, cols)
    t_pos_col = t_cols.reshape(cols, 1)

    page_major = lambda c: jnp.transpose(c, (0, 1, 3, 4, 5, 2)).reshape(depth, n_pool, PAIR_W, PAGE_SIZE)
    cache_cmp_t = page_major(cache_cmp_kv)
    cache_sel_t = page_major(cache_sel_kv)

    m_p = batch * seq
    m_s = dbatch * dseq
    tm_p = 1024
    tm_s = m_s

    xp = x_prompt.reshape(m_p, D_MODEL)
    xs = x_sample.reshape(m_s, D_MODEL)
    st_p, st_s = [], []
    for l in range(depth):
        w_attn, w_ng, w_ret = _prep_w_in(w_in[l])
        wk, wv, pos_row = _prep_w_phi(w_phi[l], cmp_pos[l])
        lw = dict(norm_mix=norm_mix[l], norm_mlp=norm_mlp[l], w_attn=w_attn, w_ng=w_ng, w_ret=w_ret,
                  w_nsa_proj=w_nsa_proj[l].astype(bf16), w_ret_proj=w_ret_proj[l].astype(bf16),
                  w_out=w_out[l].astype(bf16), w_up=w_up[l].astype(bf16), w_down=w_down[l].astype(bf16))
        gn_w = ret_gn_w[l].reshape(1, -1)
        gn_b = ret_gn_b[l].reshape(1, -1)

        mod_p = [a.reshape(batch, 1, D_MODEL) for a in jnp.split(mod_all[l, :batch], 6, axis=-1)]

        def attend_prompt(q, kv, gates, rqk, rv, gts):
            kv_raw, k_rows, vt_rows, raw_t = kv
            kc, vct = _compress_prompt(kv_raw, pos_row, wk, wv, batch)
            o = _nsa_prompt(q, kc, vct, k_rows, vt_rows, gates, bias_c, bias_t, batch, seq)
            zero_state = jnp.zeros((batch, RET_HEADS, RET_DK, RET_DV), f32)
            r, s_new = _retention(rqk, rv, gts, gn_w, gn_b, zero_state, *dec_p, batch, seq, RET_CHUNK, bf16)
            return o, r, (raw_t[0], raw_t[1], raw_t[2][:, :, seq - min(WINDOW, seq):], s_new)

        rope_p = (cos_p, sin_p, pl.BlockSpec((tm_p, RET_DK // 2), lambda i, j: (i % (seq // tm_p), 0)))
        xp, st = _layer(xp, mod_p, lw, seq, tm_p, bf16, HEAD_DIM ** -0.5 * LOG2E, True, rope_p, attend_prompt)
        st_p.append(st)

        mod_rows = jnp.repeat(mod_all[l, batch:batch + dbatch], dseq, axis=0)
        mod_s = jnp.split(mod_rows, 6, axis=-1)

        def attend_sample(q, kv_raw, gates, rqk, rv, gts):
            cmp_new, sel_new, win_new = (kv_raw[j].reshape(dbatch, dseq, PAIR_W) for j in range(3))
            kc, vct = _compress_sample(cache_cmp_t, l, page_table, pos_row, wk, wv)
            q5 = q.reshape(dbatch, dseq, N_KV, HPG, HEAD_DIM)
            qp = jnp.einsum('btghd,gk->bhgtkd', q5, jnp.eye(N_KV, dtype=q.dtype)).reshape(
                dbatch, cols, KV_W).astype(bf16)
            oc_t, sel_mask = _nsa_sample_cmp(kc, vct, jnp.swapaxes(qp, 1, 2), bias_c_s, t_pos_row,
                                             n_sel_s, n_sel_pad)
            g4 = gates[:, :N_KV * NG_SLOT].reshape(dbatch, dseq, N_KV, NG_SLOT)[..., :3 * HPG]
            g4 = g4.reshape(dbatch, dseq, N_KV, HPG, 3)
            gates_c = jnp.transpose(g4, (0, 3, 2, 1, 4)).reshape(dbatch, cols, 3)
            gates_c = jnp.pad(gates_c, ((0, 0), (0, 0), (0, LANES - 3)))
            sel_pad = jnp.pad(sel_new, ((0, 0), (0, PAGE_SIZE - dseq), (0, 0)))
            win_all = jnp.concatenate([state_win_kv[l].reshape(dbatch, w_buf, PAIR_W), win_new], axis=1)
            win_pad = jnp.pad(win_all, ((0, 0), (0, n_win_pad - w_buf - dseq), (0, 0)))
            o_q = _nsa_sample_sel(cache_sel_t, l, page_table, sel_pad, win_pad, qp, sel_mask, oc_t,
                                  gates_c, bias_s_s, bias_new_s, bias_w_s, t_pos_col)
            o6 = o_q.reshape(dbatch, HPG, N_KV, dseq, N_KV, HEAD_DIM)
            o = jnp.einsum('bhgtgd->btghd', o6).reshape(m_s, N_HEADS * HEAD_DIM)
            r, s_new = _retention(rqk, rv, gts, gn_w, gn_b, state_ret[l], *dec_s, dbatch, dseq, dseq, f32)
            kv6 = lambda a: a.reshape(dbatch, -1, 2, N_KV, HEAD_DIM)
            return o, r, (kv6(cmp_new), kv6(sel_new), kv6(win_all[:, -w_buf:]), s_new)

        rope_s = (cos_s, sin_s, pl.BlockSpec((tm_s, RET_DK // 2), lambda i, j: (i, 0)))
        xs, st = _layer(xs, mod_s, lw, dseq, tm_s, f32, HEAD_DIM ** -0.5, False, rope_s, attend_sample)
        st_s.append(st)

    y_p = _norm("norm_final", xp, norm_final, None, None, seq, f32).reshape(batch, seq, D_MODEL)
    y_s = _norm("norm_final", xs, norm_final, None, None, dseq, f32).reshape(dbatch, dseq, D_MODEL)
    cmp_p, sel_p, win_p, ret_p = [jnp.stack([st[i] for st in st_p]) for i in range(4)]
    rows_last = lambda a: jnp.transpose(a.reshape(depth, batch, 2, N_KV, HEAD_DIM, a.shape[-1]), (0, 1, 5, 2, 3, 4))
    cmp_p, sel_p, win_p = rows_last(cmp_p), rows_last(sel_p), rows_last(win_p)
    cmp_s, sel_s, win_s, ret_s = [jnp.stack([st[i] for st in st_s]) for i in range(4)]
    return (y_p, y_s, cmp_p, sel_p, win_p, ret_p, cmp_s, sel_s, win_s, ret_s)
```

```python
import functools
import math

import jax
import jax.numpy as jnp
from jax import lax
from jax.experimental import pallas as pl
from jax.experimental.pallas import tpu as pltpu

D_MODEL = 2048
DEPTH = 4
PAST_LEN = 16384
PAGE_SIZE = 128
N_HEADS = 16
HEAD_DIM = 64
N_KV = 4
HPG = N_HEADS // N_KV
CMP_BLOCK = 32
SEL_BLOCK = 64
TOP_K = 16
WINDOW = 512
RET_HEADS = 4
RET_DK = 256
RET_DV = 512
RET_CHUNK = 128
ROPE_BASE = 10000.0
D_FF = 4 * D_MODEL
N_BUCKETS = 32
REL_MAX_DIST = 2048
NORM_EPS = 1e-6
FORCE_BONUS = 1e6
NEG = -1e30

KV_W = N_KV * HEAD_DIM
PAIR_W = 2 * KV_W
NG_SLOT = 16
NG_W = 128

LANES = 128
SUBLANES = 8
VMEM_LIMIT_BYTES = 56 * 1024 * 1024

QB = 128
KB = 128
N_BIAS_TILES = 14
SEL_SUB = 8
LINEAR_TN = 1024
LINEAR_TK = 2048
LOG2E = math.log2(math.e)

f32 = jnp.float32
bf16 = jnp.bfloat16


def _cparams(sem):
    return pltpu.CompilerParams(dimension_semantics=sem, vmem_limit_bytes=VMEM_LIMIT_BYTES)


def _linear(name, xs, ws, extras, extra_specs, epilogue, out_shapes, out_specs, tm, tn, tk=None, n=None, w_col0=0):
    m = xs[0].shape[0]
    n = ws[0].shape[1] if n is None else n
    nx, ne = len(xs), len(extras)
    if tk is None:
        def body(*refs):
            x_refs, w_refs = refs[:nx], refs[nx:2 * nx]
            e_refs = refs[2 * nx:2 * nx + ne]
            o_refs = refs[2 * nx + ne:]
            accs = [jnp.dot(x[...].astype(bf16), w[...], preferred_element_type=f32)
                    for x, w in zip(x_refs, w_refs)]
            epilogue(accs, e_refs, o_refs)

        in_specs = ([pl.BlockSpec((tm, x.shape[1]), lambda i, j: (i, 0)) for x in xs]
                    + [pl.BlockSpec((w.shape[0], tn), lambda i, j: (0, j + w_col0)) for w in ws]
                    + list(extra_specs))
        return pl.pallas_call(
            body, grid=(m // tm, n // tn), in_specs=in_specs, out_specs=out_specs,
            out_shape=out_shapes, compiler_params=_cparams(("parallel", "arbitrary")),
            name=name)(*xs, *ws, *extras)

    assert nx == 1
    nk = xs[0].shape[1] // tk

    def body_k(x_ref, w_ref, *rest):
        e_refs, o_refs, acc_ref = rest[:ne], rest[ne:-1], rest[-1]
        k = pl.program_id(2)
        part = jnp.dot(x_ref[...].astype(bf16), w_ref[...], preferred_element_type=f32)

        @pl.when(k == 0)
        def _():
            acc_ref[...] = part

        @pl.when(k > 0)
        def _():
            acc_ref[...] += part

        @pl.when(k == nk - 1)
        def _():
            epilogue([acc_ref[...]], e_refs, o_refs)

    ij = lambda spec: pl.BlockSpec(spec.block_shape, lambda i, j, k, f=spec.index_map: f(i, j))
    single = not isinstance(out_specs, (list, tuple))
    outs = ij(out_specs) if single else [ij(s) for s in out_specs]
    in_specs = ([pl.BlockSpec((tm, tk), lambda i, j, k: (i, k)), pl.BlockSpec((tk, tn), lambda i, j, k: (k, j))]
                + [ij(s) for s in extra_specs])
    return pl.pallas_call(
        body_k, grid=(m // tm, n // tn, nk), in_specs=in_specs, out_specs=outs, out_shape=out_shapes,
        scratch_shapes=[pltpu.VMEM((tm, tn), f32)],
        compiler_params=_cparams(("parallel", "arbitrary", "arbitrary")), name=name)(*xs, *ws, *extras)


def _tile_spec(tm, tn, col0=0):
    return pl.BlockSpec((tm, tn), lambda i, j: (i, j + col0))


def _mod_spec(arr, tm, tn, rows_per_batch):
    if arr.ndim == 3:
        return pl.BlockSpec((None, 1, tn), lambda i, j: ((i * tm) // rows_per_batch, 0, j))
    return pl.BlockSpec((tm, tn), lambda i, j: (i, j))


def _mod_all(c_all, w_mod, b_mod):
    depth, d, n = w_mod.shape
    tn = 1024

    def body(c_ref, w_ref, b_ref, o_ref):
        c = c_ref[...]
        a = (c * jax.nn.sigmoid(c)).astype(bf16)
        o_ref[...] = jnp.dot(a, w_ref[...].astype(bf16), preferred_element_type=f32) + b_ref[...]

    return pl.pallas_call(
        body, grid=(depth, n // tn),
        in_specs=[pl.BlockSpec(c_all.shape, lambda l, j: (0, 0)),
                  pl.BlockSpec((None, d, tn), lambda l, j: (l, 0, j)),
                  pl.BlockSpec((None, 1, tn), lambda l, j: (l, 0, j))],
        out_specs=pl.BlockSpec((None, c_all.shape[0], tn), lambda l, j: (l, 0, j)),
        out_shape=jax.ShapeDtypeStruct((depth, c_all.shape[0], n), f32),
        compiler_params=_cparams(("parallel", "parallel")), name="mod_all",
    )(c_all, w_mod, b_mod.reshape(depth, 1, n))


def _norm(name, x, gain, sc, sh, rows_per_batch, out_dtype):
    m, d = x.shape
    tm = min(256, m)
    has_mod = sc is not None

    def body(*refs):
        if has_mod:
            x_ref, g_ref, sc_ref, sh_ref, o_ref = refs
        else:
            x_ref, g_ref, o_ref = refs
        xf = x_ref[...]
        y = xf * lax.rsqrt(jnp.mean(xf * xf, axis=-1, keepdims=True) + NORM_EPS) * g_ref[...]
        if has_mod:
            y = y * (1.0 + sc_ref[...]) + sh_ref[...]
        o_ref[...] = y.astype(out_dtype)

    in_specs = [pl.BlockSpec((tm, d), lambda i: (i, 0)), pl.BlockSpec((1, d), lambda i: (0, 0))]
    args = [x, gain.reshape(1, d)]
    if has_mod:
        for a in (sc, sh):
            if a.ndim == 3:
                in_specs.append(pl.BlockSpec((None, 1, d), lambda i: ((i * tm) // rows_per_batch, 0, 0)))
            else:
                in_specs.append(pl.BlockSpec((tm, d), lambda i: (i, 0)))
            args.append(a)
    return pl.pallas_call(
        body, grid=(m // tm,), in_specs=in_specs,
        out_specs=pl.BlockSpec((tm, d), lambda i: (i, 0)),
        out_shape=jax.ShapeDtypeStruct((m, d), out_dtype),
        compiler_params=_cparams(("parallel",)), name=name)(*args)


def _t5_bucket(dist):
    n = jnp.maximum(dist, 0)
    exact = N_BUCKETS // 2
    nf = jnp.maximum(n, exact).astype(f32)
    large = exact + (jnp.log(nf / exact) / math.log(REL_MAX_DIST / exact) * (N_BUCKETS - exact)).astype(jnp.int32)
    large = jnp.minimum(large, N_BUCKETS - 1)
    return jnp.where(n < exact, n, large)


def _bias_from_dist(dist, rel_bias, scale=1.0):
    r, c = dist.shape
    tr = next(t for t in (512, 256, 128, 64, 32, 16, 8) if r % t == 0)

    def body(tab_ref, d_ref, o_ref):
        h = pl.program_id(0)
        bucket = _t5_bucket(d_ref[...])
        acc = jnp.zeros(bucket.shape, f32)
        for k in range(N_BUCKETS):
            acc = jnp.where(bucket == k, tab_ref[k, h], acc)
        o_ref[...] = acc * scale

    return pl.pallas_call(
        body, grid=(N_HEADS, r // tr),
        in_specs=[pl.BlockSpec(memory_space=pltpu.SMEM),
                  pl.BlockSpec((tr, c), lambda h, i: (i, 0))],
        out_specs=pl.BlockSpec((None, tr, c), lambda h, i: (h, i, 0)),
        out_shape=jax.ShapeDtypeStruct((N_HEADS, r, c), f32),
        compiler_params=_cparams(("parallel", "parallel")), name="t5_bias",
    )(rel_bias, dist)


def _compress_accumulate(load_rows, pos_ref, wk_ref, wv_ref, rows):
    acc_k = jnp.zeros((rows, KV_W), f32)
    acc_v = jnp.zeros((rows, KV_W), f32)
    for c in range(CMP_BLOCK):
        lo = c * PAIR_W
        xk = (load_rows(c, 0) + pos_ref[:, lo:lo + KV_W]).astype(bf16)
        xv = (load_rows(c, 1) + pos_ref[:, lo + KV_W:lo + PAIR_W]).astype(bf16)
        acc_k = acc_k + jnp.dot(xk, wk_ref[c], preferred_element_type=f32)
        acc_v = acc_v + jnp.dot(xv, wv_ref[c], preferred_element_type=f32)
    return acc_k, acc_v


def _compress_prompt(kv_raw, pos_row, wk, wv, batch):
    seq = kv_raw.shape[1] // batch
    n = seq // CMP_BLOCK
    n_lane_tiles = PAIR_W // LANES

    def body(*refs):
        x_refs = refs[:n_lane_tiles]
        pos_ref, wk_ref, wv_ref, kc_ref, vct_ref = refs[n_lane_tiles:]

        def load_rows(c, kv):
            tiles = [x_refs[kv * (KV_W // LANES) + j][pl.ds(c, n, stride=CMP_BLOCK), :]
                     for j in range(KV_W // LANES)]
            return jnp.concatenate(tiles, axis=1)
        acc_k, acc_v = _compress_accumulate(load_rows, pos_ref, wk_ref, wv_ref, n)
        kc_ref[...] = acc_k.astype(bf16)
        vct_ref[...] = acc_v.T.astype(bf16)

    lane_tile = lambda j: pl.BlockSpec((None, seq, LANES), lambda b: (0, b, j))
    return pl.pallas_call(
        body, grid=(batch,),
        in_specs=[lane_tile(j) for j in range(n_lane_tiles)]
        + [pl.BlockSpec(pos_row.shape, lambda b: (0, 0)),
           pl.BlockSpec(wk.shape, lambda b: (0, 0, 0)),
           pl.BlockSpec(wv.shape, lambda b: (0, 0, 0))],
        out_specs=[pl.BlockSpec((None, n, KV_W), lambda b: (b, 0, 0)),
                   pl.BlockSpec((None, KV_W, n), lambda b: (b, 0, 0))],
        out_shape=[jax.ShapeDtypeStruct((batch, n, KV_W), bf16),
                   jax.ShapeDtypeStruct((batch, KV_W, n), bf16)],
        compiler_params=_cparams(("parallel",)), name="compress_prompt",
    )(*([kv_raw] * n_lane_tiles), pos_row, wk, wv)


CMP_PAGES_PER_STEP = 32


def _gather_pages(pt_ref, cache_ref, xbuf, sems, layer, pp, steps, n_steps):
    g = pl.program_id(0) * steps + pl.program_id(1)
    slot = g % 2

    def copies(step, into):
        row, first = step // steps, (step % steps) * pp
        return [pltpu.make_async_copy(cache_ref.at[layer, pt_ref[row, first + p]], xbuf.at[into, p],
                                      sems.at[into, p]) for p in range(pp)]

    @pl.when(g == 0)
    def _():
        for c in copies(g, slot):
            c.start()

    @pl.when(g + 1 < n_steps)
    def _():
        for c in copies(g + 1, 1 - slot):
            c.start()

    for c in copies(g, slot):
        c.wait()
    return slot


CMP_ROW_PITCH = CMP_BLOCK + SUBLANES


def _compress_sample(cache_t, layer, page_table, pos_row, wk, wv):
    batch, n_pages = page_table.shape
    blocks_per_page = PAGE_SIZE // CMP_BLOCK
    pp = CMP_PAGES_PER_STEP
    steps = n_pages // pp
    rows = pp * blocks_per_page
    n_lane_tiles = PAIR_W // LANES
    page_pitch = blocks_per_page * CMP_ROW_PITCH

    def body(pt_ref, cache_ref, pos_ref, wk_ref, wv_ref, kc_ref, vct_ref, xbuf, sems, tscr):
        slot = _gather_pages(pt_ref, cache_ref, xbuf, sems, layer, pp, steps, batch * steps)

        def untranspose(p, carry):
            base = pl.multiple_of(p * page_pitch, SUBLANES)
            for j in range(n_lane_tiles):
                tile = xbuf[slot, p, j * LANES:(j + 1) * LANES, :].T
                for n in range(blocks_per_page):
                    tscr[j, pl.ds(base + n * CMP_ROW_PITCH, CMP_BLOCK), :] = tile[n * CMP_BLOCK:(n + 1) * CMP_BLOCK]
            return carry
        lax.fori_loop(0, pp, untranspose, 0)

        def load_rows(c, kv):
            tiles = [tscr[kv * (KV_W // LANES) + j, pl.ds(c, rows, stride=CMP_ROW_PITCH), :]
                     for j in range(KV_W // LANES)]
            return jnp.concatenate(tiles, axis=1)
        acc_k, acc_v = _compress_accumulate(load_rows, pos_ref, wk_ref, wv_ref, rows)
        kc_ref[...] = acc_k.astype(bf16)
        vct_ref[...] = acc_v.T.astype(bf16)

    grid_spec = pltpu.PrefetchScalarGridSpec(
        num_scalar_prefetch=1, grid=(batch, steps),
        in_specs=[pl.BlockSpec(memory_space=pl.ANY),
                  pl.BlockSpec(pos_row.shape, lambda b, s, pt: (0, 0)),
                  pl.BlockSpec(wk.shape, lambda b, s, pt: (0, 0, 0)),
                  pl.BlockSpec(wv.shape, lambda b, s, pt: (0, 0, 0))],
        out_specs=[pl.BlockSpec((None, rows, KV_W), lambda b, s, pt: (b, s, 0)),
                   pl.BlockSpec((None, KV_W, rows), lambda b, s, pt: (b, 0, s))],
        scratch_shapes=[pltpu.VMEM((2, pp, PAIR_W, PAGE_SIZE), f32), pltpu.SemaphoreType.DMA((2, pp)),
                        pltpu.VMEM((n_lane_tiles, pp * page_pitch, LANES), f32)])
    n_cmp = n_pages * blocks_per_page
    return pl.pallas_call(
        body, grid_spec=grid_spec,
        out_shape=[jax.ShapeDtypeStruct((batch, n_cmp, KV_W), bf16),
                   jax.ShapeDtypeStruct((batch, KV_W, n_cmp), bf16)],
        compiler_params=_cparams(("arbitrary", "arbitrary")), name="compress_sample",
    )(page_table, cache_t, pos_row, wk, wv)


def _softmax_cols(s, mask, exp=jnp.exp):
    sm = jnp.where(mask, s, NEG)
    m = jnp.max(sm, axis=0, keepdims=True)
    e = exp(sm - m)
    p = e / jnp.sum(e, axis=0, keepdims=True)
    return jnp.where(mask, p, 0.0)


def _topk_mask(score, score_scr, n_rows, k, groups):
    n_pad, c = score.shape
    per = -(-n_rows // groups)
    assert groups * per <= n_pad
    rows = lax.broadcasted_iota(jnp.int32, score.shape, 0)
    lane_grp = lax.broadcasted_iota(jnp.int32, (1, c), 1) // (c // groups)
    score_scr[...] = score

    def step(j, cnt):
        sj = score_scr[pl.ds(j, 1), :]
        for h in range(1, groups):
            sj = jnp.where(lane_grp == h, score_scr[pl.ds(j + h * per, 1), :], sj)
        jv = j + per * lane_grp
        ahead = (sj > score) | ((sj == score) & (jv < rows))
        return cnt + jnp.where(ahead, 1.0, 0.0)

    cnt = lax.fori_loop(0, per, step, jnp.zeros(score.shape, f32))
    total = cnt
    for h in range(1, groups):
        total = total + pltpu.roll(cnt, h * (c // groups), axis=1)
    return jnp.where(total < k, 1.0, 0.0)


def _topk_mask_unrolled(score, score_scr, k):
    n_rows, c = score.shape
    score_scr[...] = score
    n_tiles = n_rows // SUBLANES
    tiles = [score[SUBLANES * v:SUBLANES * (v + 1)] for v in range(n_tiles)]
    row_in_tile = lax.broadcasted_iota(jnp.int32, (SUBLANES, c), 0)
    cnt = [jnp.zeros((SUBLANES, c), f32) for _ in range(n_tiles)]
    for j in range(n_rows):
        sj = score_scr[j:j + 1, :]
        vj = j // SUBLANES
        for v in range(n_tiles):
            if v > vj:
                ahead = sj >= tiles[v]
            elif v < vj:
                ahead = sj > tiles[v]
            else:
                ahead = (sj > tiles[v]) | ((sj == tiles[v]) & (row_in_tile > j % SUBLANES))
            cnt[v] = cnt[v] + jnp.where(ahead, 1.0, 0.0)
    return jnp.concatenate([jnp.where(cv < k, 1.0, 0.0) for cv in cnt], axis=0)


def _nsa_prompt(q, kc, vct, k_rows, vt_rows, gates, bias_c, bias_t, batch, seq):
    nq = seq // QB
    n_cmp = seq // CMP_BLOCK
    n_sel = seq // SEL_BLOCK
    ratio = SEL_BLOCK // CMP_BLOCK
    k_eff = min(TOP_K, n_sel)
    wq = HPG * QB
    win_chunks = WINDOW // KB

    def body(q_ref, kc_ref, vct_ref, ks_ref, vst_ref, kw_ref, vwt_ref, g_ref, bc_ref, bt_ref,
             o_ref, imp_scr, score_scr, sel_scr, gate_scr):
        g = pl.program_id(1)
        i = pl.program_id(2)
        q0 = i * QB

        qt = q_ref[...].astype(f32).T
        qcat = jnp.concatenate([qt[h * HEAD_DIM:(h + 1) * HEAD_DIM] for h in range(HPG)], axis=1)
        qrep = jnp.concatenate([qcat] * N_KV, axis=0)
        row_grp = lax.broadcasted_iota(jnp.int32, (KV_W, wq), 0) // HEAD_DIM
        qpad = jnp.where(row_grp == g, qrep, 0.0).astype(bf16)

        t_row = q0 + lax.broadcasted_iota(jnp.int32, (1, wq), 1) % QB

        kj = lax.broadcasted_iota(jnp.int32, (KB, QB), 0)
        qi = lax.broadcasted_iota(jnp.int32, (KB, QB), 1)
        rel = qi - kj
        upper = kj < SEL_BLOCK

        def scores(k_ref, chunk, bias, mask):
            k0 = pl.multiple_of(chunk * KB, KB)
            s = jnp.dot(k_ref[pl.ds(k0, KB), :], qpad, preferred_element_type=f32)
            penalty = jnp.where(mask, 0.0, NEG)
            return s + bias + jnp.concatenate([penalty] * HPG, axis=1)

        def weighted_values(vt_ref, chunks, ps):
            return sum(jnp.dot(vt_ref[:, pl.ds(pl.multiple_of(c * KB, KB), KB)], p.astype(bf16),
                               preferred_element_type=f32) for c, p in zip(chunks, ps))

        def col_max(ss):
            return functools.reduce(jnp.maximum, [jnp.max(s, axis=0, keepdims=True) for s in ss])

        chunks, ss = [], []
        for dd in range(win_chunks + 1):
            c = i - dd
            dist = rel + (dd * KB + jnp.where(c >= 0, 0, 2 * WINDOW))
            mask = (dist >= 0) & (dist < WINDOW)
            chunks.append(jnp.maximum(c, 0))
            ss.append(scores(kw_ref, chunks[-1], bt_ref[dd], mask))
        m_w = col_max(ss)
        ps = [jnp.exp2(s - m_w) for s in ss]
        l_w = sum(jnp.sum(p, axis=0, keepdims=True) for p in ps)
        o_w = weighted_values(vwt_ref, chunks, ps) / l_w

        s_c = jnp.dot(kc_ref[...], qpad, preferred_element_type=f32) + bc_ref[...]
        c_end = lax.broadcasted_iota(jnp.int32, (n_cmp, wq), 0) * CMP_BLOCK + (CMP_BLOCK - 1)
        p_c = _softmax_cols(s_c, c_end <= t_row, jnp.exp2)
        o_c = jnp.dot(vct_ref[...], p_c.astype(bf16), preferred_element_type=f32)

        imp_scr[...] = sum(p_c[:, h * QB:(h + 1) * QB] for h in range(HPG))
        imp = sum(imp_scr[pl.ds(r, n_sel, stride=ratio), :] for r in range(ratio))
        blk = lax.broadcasted_iota(jnp.int32, (n_sel, QB), 0)
        t_q = q0 + lax.broadcasted_iota(jnp.int32, (n_sel, QB), 1)
        cur = t_q // SEL_BLOCK
        forced = (blk == 0) | (blk == cur) | (blk == cur - 1)
        score = imp + jnp.where(forced, FORCE_BONUS, 0.0)
        score = jnp.where(blk * SEL_BLOCK <= t_q, score, NEG)
        sel_scr[...] = _topk_mask_unrolled(score, score_scr, k_eff)

        def sel_chunks(j):
            return [jnp.minimum(j * SEL_SUB + r, i) for r in range(SEL_SUB)]

        def sel_scores(j):
            ss = []
            for r, cc in enumerate(sel_chunks(j)):
                rows2 = sel_scr[pl.ds(pl.multiple_of(cc * (KB // SEL_BLOCK), KB // SEL_BLOCK),
                                      KB // SEL_BLOCK), :]
                chosen = jnp.where(upper, rows2[0:1, :], rows2[1:2, :]) > 0.5
                mask = chosen & (rel + (i - (j * SEL_SUB + r)) * KB >= 0)
                ss.append(scores(ks_ref, cc, bt_ref[jnp.minimum(i - cc, N_BIAS_TILES - 1)], mask))
            return ss

        def sel_step(j, carry):
            m, l, acc = carry
            ss = sel_scores(j)
            m_new = jnp.maximum(m, col_max(ss))
            alpha = jnp.exp2(m - m_new)
            ps = [jnp.exp2(s - m_new) for s in ss]
            l = alpha * l + sum(jnp.sum(p, axis=0, keepdims=True) for p in ps)
            acc = alpha * acc + weighted_values(vst_ref, sel_chunks(j), ps)
            return m_new, l, acc

        init = (jnp.full((1, wq), NEG, f32), jnp.zeros((1, wq), f32), jnp.zeros((HEAD_DIM, wq), f32))
        _, l_s, acc_s = lax.fori_loop(0, (i + SEL_SUB) // SEL_SUB, sel_step, init)
        o_s = acc_s / l_s

        gate_scr[...] = g_ref[...].T
        gg = gate_scr[pl.ds(pl.multiple_of(g * NG_SLOT, NG_SLOT), NG_SLOT), :]
        outs = []
        for h in range(HPG):
            sl = slice(h * QB, (h + 1) * QB)
            outs.append(gg[3 * h:3 * h + 1] * o_c[:, sl] + gg[3 * h + 1:3 * h + 2] * o_s[:, sl]
                        + gg[3 * h + 2:3 * h + 3] * o_w[:, sl])
        o_ref[...] = jnp.concatenate(outs, axis=0).T.astype(o_ref.dtype)

    return pl.pallas_call(
        body, grid=(batch, N_KV, nq),
        in_specs=[
            pl.BlockSpec((QB, KV_W), lambda b, g, i: (b * nq + i, g)),
            pl.BlockSpec((None, n_cmp, KV_W), lambda b, g, i: (b, 0, 0)),
            pl.BlockSpec((None, HEAD_DIM, n_cmp), lambda b, g, i: (b, g, 0)),
            pl.BlockSpec((None, seq, KV_W), lambda b, g, i: (1, b, 0)),
            pl.BlockSpec((None, None, HEAD_DIM, seq), lambda b, g, i: (1, b, g, 0)),
            pl.BlockSpec((None, seq, KV_W), lambda b, g, i: (2, b, 0)),
            pl.BlockSpec((None, None, HEAD_DIM, seq), lambda b, g, i: (2, b, g, 0)),
            pl.BlockSpec((QB, NG_W), lambda b, g, i: (b * nq + i, 0)),
            pl.BlockSpec((None, None, n_cmp, wq), lambda b, g, i: (i, g, 0, 0)),
            pl.BlockSpec((N_BIAS_TILES, None, KB, wq), lambda b, g, i: (0, g, 0, 0)),
        ],
        out_specs=pl.BlockSpec((QB, KV_W), lambda b, g, i: (b * nq + i, g)),
        out_shape=jax.ShapeDtypeStruct((batch * seq, N_HEADS * HEAD_DIM), bf16),
        scratch_shapes=[pltpu.VMEM((n_cmp, QB), f32), pltpu.VMEM((n_sel, QB), f32),
                        pltpu.VMEM((n_sel, QB), f32), pltpu.VMEM((NG_W, QB), f32)],
        compiler_params=_cparams(("parallel", "parallel", "arbitrary")), name="nsa_prompt",
    )(q, kc, vct, k_rows, vt_rows, k_rows, vt_rows, gates, bias_c, bias_t)


def _retention(qk, v, rg, gn_w, gn_b, state0, dec_in, dec_q, dec_k, dec_c, batch, length, chunk, out_dtype):
    n = length // chunk
    w_qk, w_v = RET_HEADS * RET_DK, RET_HEADS * RET_DV

    def body(q_ref, k_ref, v_ref, rg_ref, gw_ref, gb_ref, s0_ref, din_ref, dq_ref, dk_ref, dc_ref,
             r_ref, s_ref):
        @pl.when(pl.program_id(1) == 0)
        def _():
            s_ref[...] = s0_ref[...]

        for h in range(RET_HEADS):
            qs, vs = slice(h * RET_DK, (h + 1) * RET_DK), slice(h * RET_DV, (h + 1) * RET_DV)
            qc = q_ref[:, qs].astype(bf16)
            kf = k_ref[:, qs].astype(f32)
            vc = v_ref[:, vs].astype(bf16)
            st = s_ref[h]
            inner = lax.dot_general(qc, kf.astype(bf16), (((1,), (1,)), ((), ())),
                                    preferred_element_type=f32) * din_ref[h]
            o = jnp.dot(inner.astype(bf16), vc, preferred_element_type=f32)
            o = o + jnp.dot(qc, st.astype(bf16), preferred_element_type=f32) * dq_ref[h]
            kd = (kf * dk_ref[h]).astype(bf16)
            s_ref[h] = st * dc_ref[h] + lax.dot_general(kd, vc, (((0,), (0,)), ((), ())),
                                                        preferred_element_type=f32)
            mu = jnp.mean(o, axis=-1, keepdims=True)
            var = jnp.mean(jnp.square(o - mu), axis=-1, keepdims=True)
            y = (o - mu) * lax.rsqrt(var + NORM_EPS) * gw_ref[:, vs] + gb_ref[:, vs]
            r_ref[:, vs] = (rg_ref[:, vs].astype(f32) * y).astype(r_ref.dtype)

    seq_spec = lambda w, col0=0: pl.BlockSpec((chunk, w), lambda b, c: (b * n + c, col0))
    full_spec = lambda a: pl.BlockSpec(a.shape, lambda b, c: (0,) * a.ndim)
    state_spec = pl.BlockSpec((None, RET_HEADS, RET_DK, RET_DV), lambda b, c: (b, 0, 0, 0))
    return pl.pallas_call(
        body, grid=(batch, n),
        in_specs=[seq_spec(w_qk), seq_spec(w_qk, 1), seq_spec(w_v), seq_spec(w_v),
                  full_spec(gn_w), full_spec(gn_b), state_spec,
                  full_spec(dec_in), full_spec(dec_q), full_spec(dec_k), full_spec(dec_c)],
        out_specs=[seq_spec(w_v), state_spec],
        out_shape=[jax.ShapeDtypeStruct((batch * length, w_v), out_dtype),
                   jax.ShapeDtypeStruct((batch, RET_HEADS, RET_DK, RET_DV), f32)],
        compiler_params=_cparams(("parallel", "arbitrary")), name="retention",
    )(qk, qk, v, rg, gn_w, gn_b, state0, dec_in, dec_q, dec_k, dec_c)


def _retention_decays(chunk):
    lg = jnp.log(1.0 - jnp.exp2(-5.0 - jnp.arange(RET_HEADS, dtype=f32)))
    i = jnp.arange(chunk, dtype=f32)
    diff = i[:, None] - i[None, :]
    dec_in = jnp.where(diff >= 0, jnp.exp(jnp.maximum(diff, 0.0)[None] * lg[:, None, None]), 0.0)
    dec_q = jnp.exp((i[None, :] + 1.0) * lg[:, None])[..., None]
    dec_k = jnp.exp((chunk - 1.0 - i)[None, :] * lg[:, None])[..., None]
    dec_c = jnp.exp(chunk * lg)[:, None, None]
    return dec_in, dec_q, dec_k, dec_c


def _rope_tables(pos):
    half = RET_DK // 2
    inv = ROPE_BASE ** (-jnp.arange(half, dtype=f32) / half)
    ang = pos.astype(f32)[:, None] * inv[None, :]
    return jnp.cos(ang), jnp.sin(ang)


def _nsa_sample_cmp(kc, vct, qpt, bias_c, t_pos_row, n_sel, n_sel_pad):
    batch, n_cmp, _ = kc.shape
    ratio = SEL_BLOCK // CMP_BLOCK
    cols = qpt.shape[2]
    grp_cols = cols // HPG

    def body(kc_ref, vct_ref, q_ref, bc_ref, t_ref, oc_ref, sel_ref, imp_scr, score_scr):
        t_row = t_ref[...]
        s_c = jnp.dot(kc_ref[...], q_ref[...], preferred_element_type=f32) + bc_ref[...]
        c_end = lax.broadcasted_iota(jnp.int32, (n_cmp, cols), 0) * CMP_BLOCK + (CMP_BLOCK - 1)
        p_c = _softmax_cols(s_c, c_end <= t_row)
        oc_ref[...] = jnp.dot(vct_ref[...], p_c.astype(bf16), preferred_element_type=f32)
        assert HPG == 4
        pair = p_c + pltpu.roll(p_c, 2 * grp_cols, axis=1)
        imp_rep = pair + pltpu.roll(pair, grp_cols, axis=1)
        imp_scr[...] = jnp.zeros(imp_scr.shape, f32)
        imp_scr[pl.ds(0, n_cmp), :] = imp_rep
        imp = sum(imp_scr[pl.ds(r, n_sel_pad, stride=ratio), :] for r in range(ratio))
        blk = lax.broadcasted_iota(jnp.int32, (n_sel_pad, cols), 0)
        cur = t_row // SEL_BLOCK
        forced = (blk == 0) | (blk == cur) | (blk == cur - 1)
        score = imp + jnp.where(forced, FORCE_BONUS, 0.0)
        score = jnp.where(blk * SEL_BLOCK <= t_row, score, NEG)
        score = jnp.where(blk < n_sel, score, -jnp.inf)
        sel_ref[...] = _topk_mask(score, score_scr, n_sel, min(TOP_K, n_sel), HPG)

    return pl.pallas_call(
        body, grid=(batch,),
        in_specs=[pl.BlockSpec((None, n_cmp, KV_W), lambda b: (b, 0, 0)),
                  pl.BlockSpec((None, KV_W, n_cmp), lambda b: (b, 0, 0)),
                  pl.BlockSpec((None, KV_W, cols), lambda b: (b, 0, 0)),
                  pl.BlockSpec((n_cmp, cols), lambda b: (0, 0)),
                  pl.BlockSpec((1, cols), lambda b: (0, 0))],
        out_specs=[pl.BlockSpec((None, KV_W, cols), lambda b: (b, 0, 0)),
                   pl.BlockSpec((None, n_sel_pad, cols), lambda b: (b, 0, 0))],
        out_shape=[jax.ShapeDtypeStruct((batch, KV_W, cols), f32),
                   jax.ShapeDtypeStruct((batch, n_sel_pad, cols), f32)],
        scratch_shapes=[pltpu.VMEM((n_sel_pad * ratio, cols), f32), pltpu.VMEM((n_sel_pad, cols), f32)],
        compiler_params=_cparams(("parallel",)), name="nsa_sample_cmp",
    )(kc, vct, qpt, bias_c, t_pos_row)


SEL_PAGES_PER_STEP = 16


def _softmax_rows(s, mask):
    sm = jnp.where(mask, s, NEG)
    m = jnp.max(sm, axis=1, keepdims=True)
    e = jnp.exp(sm - m)
    p = e / jnp.sum(e, axis=1, keepdims=True)
    return jnp.where(mask, p, 0.0)


def _nsa_sample_sel(cache_t, layer, page_table, sel_new, win_all, qp, sel_mask, oc_t, gates_c,
                    bias_s, bias_new, bias_w, t_pos_col):
    batch, n_pages = page_table.shape
    pp = SEL_PAGES_PER_STEP
    steps = n_pages // pp
    keys = pp * PAGE_SIZE
    nq = qp.shape[1]
    blocks_per_page = PAGE_SIZE // SEL_BLOCK
    nb = pp * blocks_per_page
    n_win_pad = win_all.shape[1]
    w_buf = min(WINDOW, PAST_LEN)
    nt = (((1,), (1,)), ((), ()))
    tn = (((0,), (0,)), ((), ()))

    def body(pt_ref, cache_ref, new_ref, win_ref, q_ref, selm_ref, oc_ref, g_ref, bs_ref, bn_ref, bw_ref,
             t_ref, o_ref, xbuf, sems, m_scr, l_scr, acc_scr):
        s = pl.program_id(1)
        slot = _gather_pages(pt_ref, cache_ref, xbuf, sems, layer, pp, steps, batch * steps)

        @pl.when(s == 0)
        def _():
            m_scr[...] = jnp.full(m_scr.shape, NEG, f32)
            l_scr[...] = jnp.zeros(l_scr.shape, f32)
            acc_scr[...] = jnp.zeros(acc_scr.shape, f32)

        q = q_ref[...]
        t_col = t_ref[...]

        def chosen_keys(block_rows, n_keys):
            n_blocks = block_rows.shape[0]
            key_blk = lax.broadcasted_iota(jnp.int32, (n_blocks, n_keys), 1) // SEL_BLOCK
            spread = jnp.where(key_blk == lax.broadcasted_iota(jnp.int32, (n_blocks, n_keys), 0), 1.0, 0.0)
            return lax.dot_general(block_rows, spread, tn, preferred_element_type=f32) > 0.5

        def online(s_qk, mask, weighted_values):
            s_m = jnp.where(mask, s_qk, NEG)
            m_old = m_scr[...]
            m_new = jnp.maximum(m_old, jnp.max(s_m, axis=1, keepdims=True))
            alpha = jnp.exp(m_old - m_new)
            p = jnp.where(mask, jnp.exp(s_m - m_new), 0.0)
            l_scr[...] = alpha * l_scr[...] + jnp.sum(p, axis=1, keepdims=True)
            acc_scr[...] = alpha * acc_scr[...] + weighted_values(p.astype(bf16))
            m_scr[...] = m_new

        s_qk = jnp.concatenate(
            [jnp.dot(q, xbuf[slot, p, :KV_W, :].astype(bf16), preferred_element_type=f32) for p in range(pp)],
            axis=1) + bs_ref[...]
        mask = chosen_keys(selm_ref[pl.ds(pl.multiple_of(s * nb, nb), nb), :], keys)
        online(s_qk, mask, lambda pb: sum(
            lax.dot_general(pb[:, p * PAGE_SIZE:(p + 1) * PAGE_SIZE], xbuf[slot, p, KV_W:, :].astype(bf16), nt,
                            preferred_element_type=f32) for p in range(pp)))

        @pl.when(s == steps - 1)
        def _():
            kv_n = new_ref[...]
            s_n = lax.dot_general(q, kv_n[:, :KV_W].astype(bf16), nt, preferred_element_type=f32) + bn_ref[...]
            pos_n = PAST_LEN + lax.broadcasted_iota(jnp.int32, (nq, PAGE_SIZE), 1)
            mask_n = chosen_keys(selm_ref[pl.ds(n_pages * blocks_per_page, SUBLANES), :], PAGE_SIZE) & (pos_n <= t_col)
            online(s_n, mask_n, lambda pb: jnp.dot(pb, kv_n[:, KV_W:].astype(bf16), preferred_element_type=f32))
            o_s = acc_scr[...] / l_scr[...]

            kv_w = win_ref[...]
            kpos = (PAST_LEN - w_buf) + lax.broadcasted_iota(jnp.int32, (nq, n_win_pad), 1)
            dist = t_col - kpos
            mask_w = (dist >= 0) & (dist < WINDOW) & (kpos >= 0)
            s_w = lax.dot_general(q, kv_w[:, :KV_W].astype(bf16), nt, preferred_element_type=f32) + bw_ref[...]
            p_w = _softmax_rows(s_w, mask_w)
            o_w = jnp.dot(p_w.astype(bf16), kv_w[:, KV_W:].astype(bf16), preferred_element_type=f32)
            gts = g_ref[...]
            o_ref[...] = gts[:, 0:1] * oc_ref[...].T + gts[:, 1:2] * o_s + gts[:, 2:3] * o_w

    grid_spec = pltpu.PrefetchScalarGridSpec(
        num_scalar_prefetch=1, grid=(batch, steps),
        in_specs=[pl.BlockSpec(memory_space=pl.ANY),
                  pl.BlockSpec((None, PAGE_SIZE, PAIR_W), lambda b, s, pt: (b, 0, 0)),
                  pl.BlockSpec((None, n_win_pad, PAIR_W), lambda b, s, pt: (b, 0, 0)),
                  pl.BlockSpec((None, nq, KV_W), lambda b, s, pt: (b, 0, 0)),
                  pl.BlockSpec((None, sel_mask.shape[1], nq), lambda b, s, pt: (b, 0, 0)),
                  pl.BlockSpec((None, KV_W, nq), lambda b, s, pt: (b, 0, 0)),
                  pl.BlockSpec((None, nq, LANES), lambda b, s, pt: (b, 0, 0)),
                  pl.BlockSpec((nq, keys), lambda b, s, pt: (0, s)),
                  pl.BlockSpec((nq, PAGE_SIZE), lambda b, s, pt: (0, 0)),
                  pl.BlockSpec((nq, n_win_pad), lambda b, s, pt: (0, 0)),
                  pl.BlockSpec((nq, 1), lambda b, s, pt: (0, 0))],
        out_specs=pl.BlockSpec((None, nq, KV_W), lambda b, s, pt: (b, 0, 0)),
        scratch_shapes=[pltpu.VMEM((2, pp, PAIR_W, PAGE_SIZE), f32), pltpu.SemaphoreType.DMA((2, pp)),
                        pltpu.VMEM((nq, 1), f32), pltpu.VMEM((nq, 1), f32),
                        pltpu.VMEM((nq, KV_W), f32)])
    return pl.pallas_call(
        body, grid_spec=grid_spec,
        out_shape=jax.ShapeDtypeStruct((batch, nq, KV_W), f32),
        compiler_params=_cparams(("arbitrary", "arbitrary")), name="nsa_sample_sel",
    )(page_table, cache_t, sel_new, win_all, qp, sel_mask, oc_t, gates_c, bias_s, bias_new, bias_w,
      t_pos_col)


_OFF_Q = 0
_OFF_KV = N_HEADS * HEAD_DIM
_OFF_NG = _OFF_KV + 6 * KV_W
_OFF_RQ = _OFF_NG + 3 * N_HEADS
_OFF_RV = _OFF_RQ + 2 * RET_HEADS * RET_DK
_OFF_RG = _OFF_RV + RET_HEADS * RET_DV
_D_IN = _OFF_RG + RET_HEADS * RET_DV + 2 * D_MODEL


def _prep_w_in(w_in_l):
    w_attn = w_in_l[:, :_OFF_NG].astype(bf16)
    ng = w_in_l[:, _OFF_NG:_OFF_RQ].astype(bf16).reshape(D_MODEL, N_KV, 3 * HPG)
    w_ng = jnp.pad(ng, ((0, 0), (0, 0), (0, NG_SLOT - 3 * HPG))).reshape(D_MODEL, N_KV * NG_SLOT)
    w_ng = jnp.pad(w_ng, ((0, 0), (0, NG_W - N_KV * NG_SLOT)))
    w_ret = w_in_l[:, _OFF_RQ:].astype(bf16)
    return w_attn, w_ng, w_ret


def _prep_w_phi(w_phi_l, cmp_pos_l):
    eye = jnp.eye(N_KV, dtype=f32)
    wk = jnp.einsum('gh,cde->cgdhe', eye, w_phi_l[0]).reshape(CMP_BLOCK, KV_W, KV_W).astype(bf16)
    wv = jnp.einsum('gh,cde->cgdhe', eye, w_phi_l[1]).reshape(CMP_BLOCK, KV_W, KV_W).astype(bf16)
    pos = jnp.broadcast_to(cmp_pos_l[:, :, None, :], (2, CMP_BLOCK, N_KV, HEAD_DIM))
    pos_row = jnp.transpose(pos, (1, 0, 2, 3)).reshape(1, CMP_BLOCK * PAIR_W)
    return wk, wv, pos_row


def _layer(x, mod, lw, rows_per_batch, tm, act, q_scale, kv_layouts, rope, attend):
    m = x.shape[0]
    sh1, sc1, g1, sh2, sc2, g2 = mod
    h = _norm("norm_mix", x, lw["norm_mix"], sc1, sh1, rows_per_batch, act)

    def plain(dtype, scale=1.0):
        def epi(accs, e_refs, o_refs):
            o_refs[0][...] = (accs[0] * scale).astype(dtype)
        return epi

    w_attn, w_ng, w_ret = lw["w_attn"], lw["w_ng"], lw["w_ret"]
    tn = LINEAR_TN
    q = _linear("proj_q", [h], [w_attn], [], [], plain(act, q_scale),
                jax.ShapeDtypeStruct((m, _OFF_KV), act), _tile_spec(tm, tn), tm, tn, n=_OFF_KV)
    kv_cols = dict(n=3 * PAIR_W, w_col0=_OFF_KV // PAIR_W)
    pair_spec = lambda w: pl.BlockSpec((None, tm, w), lambda i, j: (j, i, 0))
    if kv_layouts:
        n_b, per_b = m // rows_per_batch, rows_per_batch // tm
        t_spec = lambda w: pl.BlockSpec((None, None, w, tm), lambda i, j: (j, i // per_b, 0, i % per_b))

        def kv_epi(accs, e_refs, o_refs):
            a = accs[0]
            a_t = a.T
            o_refs[0][...] = a
            o_refs[1][...] = a[:, :KV_W].astype(bf16)
            o_refs[2][...] = a_t[KV_W:].astype(bf16)
            o_refs[3][...] = a_t
        kv_raw = _linear("proj_kv", [h], [w_attn], [], [], kv_epi,
                         [jax.ShapeDtypeStruct((3, m, PAIR_W), f32), jax.ShapeDtypeStruct((3, m, KV_W), bf16),
                          jax.ShapeDtypeStruct((3, n_b, KV_W, rows_per_batch), bf16),
                          jax.ShapeDtypeStruct((3, n_b, PAIR_W, rows_per_batch), f32)],
                         [pair_spec(PAIR_W), pair_spec(KV_W), t_spec(KV_W), t_spec(PAIR_W)], tm, PAIR_W,
                         **kv_cols)
    else:
        kv_raw = _linear("proj_kv", [h], [w_attn], [], [], plain(f32),
                         jax.ShapeDtypeStruct((3, m, PAIR_W), f32), pair_spec(PAIR_W), tm, PAIR_W, **kv_cols)

    def sig_epi(accs, e_refs, o_refs):
        o_refs[0][...] = jax.nn.sigmoid(accs[0])
    gates = _linear("proj_ng", [h], [w_ng], [], [], sig_epi,
                    jax.ShapeDtypeStruct((m, NG_W), f32), _tile_spec(tm, NG_W), tm, NG_W)

    cos, sin, rope_spec = rope
    n_qk_tiles = RET_HEADS * RET_DK // tn

    def rot_epi(accs, e_refs, o_refs):
        a = accs[0]
        c, s = e_refs[0][...], e_refs[1][...]
        scale = jnp.where(pl.program_id(1) >= n_qk_tiles, RET_DK ** -0.5, 1.0)
        half = RET_DK // 2
        outs = []
        for hh in range(tn // RET_DK):
            x1 = a[:, hh * RET_DK:hh * RET_DK + half]
            x2 = a[:, hh * RET_DK + half:(hh + 1) * RET_DK]
            outs += [x1 * c - x2 * s, x2 * c + x1 * s]
        o_refs[0][...] = (jnp.concatenate(outs, axis=1) * scale).astype(act)

    n_rot = 2 * RET_HEADS * RET_DK
    rqk = _linear("proj_rqk", [h], [w_ret], [cos, sin], [rope_spec, rope_spec], rot_epi,
                  jax.ShapeDtypeStruct((m, n_rot), act), _tile_spec(tm, tn), tm, tn, n=n_rot)
    n_rv = RET_HEADS * RET_DV
    rv = _linear("proj_rv", [h], [w_ret], [], [], plain(act),
                 jax.ShapeDtypeStruct((m, n_rv), act), _tile_spec(tm, tn), tm, tn, n=n_rv, w_col0=n_rot // tn)
    n_silu_tiles = n_rv // tn

    def gates_epi(accs, e_refs, o_refs):
        a = accs[0]
        sg = jax.nn.sigmoid(a)
        o_refs[0][...] = jnp.where(pl.program_id(1) < n_silu_tiles, a * sg, sg).astype(act)
    gts = _linear("proj_gates", [h], [w_ret], [], [], gates_epi,
                  jax.ShapeDtypeStruct((m, 3 * D_MODEL), act), _tile_spec(tm, tn), tm, tn,
                  n=3 * D_MODEL, w_col0=(n_rot + n_rv) // tn)

    o_nsa, r, states = attend(q, kv_raw, gates, rqk, rv, gts)

    def merge_epi(accs, e_refs, o_refs):
        o_refs[0][...] = (e_refs[0][...].astype(f32) * accs[0]
                          + e_refs[1][...].astype(f32) * accs[1]).astype(act)
    nt = D_MODEL // tn
    merged = _linear("merge", [o_nsa, r], [lw["w_nsa_proj"], lw["w_ret_proj"]], [gts, gts],
                     [_tile_spec(tm, tn, nt), _tile_spec(tm, tn, 2 * nt)], merge_epi,
                     jax.ShapeDtypeStruct((m, D_MODEL), act), _tile_spec(tm, tn), tm, tn)

    def resid_epi(accs, e_refs, o_refs):
        o_refs[0][...] = e_refs[0][...] + e_refs[1][...] * accs[0]
    x = _linear("out_proj", [merged], [lw["w_out"]], [x, g1],
                [_tile_spec(tm, tn), _mod_spec(g1, tm, tn, rows_per_batch)], resid_epi,
                jax.ShapeDtypeStruct((m, D_MODEL), f32), _tile_spec(tm, tn), tm, tn)

    h2 = _norm("norm_mlp", x, lw["norm_mlp"], sc2, sh2, rows_per_batch, act)

    def up_epi(accs, e_refs, o_refs):
        o_refs[0][...] = jnp.square(jnp.maximum(accs[0], 0.0)).astype(act)
    u = _linear("mlp_up", [h2], [lw["w_up"]], [], [], up_epi,
                jax.ShapeDtypeStruct((m, D_FF), act), _tile_spec(tm, tn), tm, tn)
    x = _linear("mlp_down", [u], [lw["w_down"]], [x, g2],
                [_tile_spec(tm, tn), _mod_spec(g2, tm, tn, rows_per_batch)], resid_epi,
                jax.ShapeDtypeStruct((m, D_MODEL), f32), _tile_spec(tm, tn), tm, tn, tk=LINEAR_TK)
    return x, states


def kernel(x_prompt, x_sample, c_prompt, c_sample, cache_cmp_kv, cache_sel_kv, state_win_kv, state_ret,
           page_table, rel_bias, w_mod, b_mod, norm_mix, norm_mlp, w_in, cmp_pos, w_phi, w_nsa_proj,
           ret_gn_w, ret_gn_b, w_ret_proj, w_out, w_up, w_down, norm_final):
    batch, seq, _ = x_prompt.shape
    dbatch, dseq, _ = x_sample.shape
    depth = w_in.shape[0]
    n_pool = cache_cmp_kv.shape[1]
    n_pages = page_table.shape[1]
    i32 = jnp.int32

    n_c = batch + dbatch
    c_all = jnp.pad(jnp.concatenate([c_prompt, c_sample], axis=0), ((0, (-n_c) % SUBLANES), (0, 0)))
    mod_all = _mod_all(c_all, w_mod, b_mod)

    nq = seq // QB
    n_cmp_p = seq // CMP_BLOCK
    kj = jnp.arange(KB, dtype=i32)[:, None]
    qi = jnp.arange(QB, dtype=i32)[None, :]
    dist_tiles = (jnp.arange(N_BIAS_TILES, dtype=i32)[:, None, None] * KB + qi[None] - kj[None])
    bias_t = _bias_from_dist(dist_tiles.reshape(N_BIAS_TILES * KB, QB), rel_bias, LOG2E)
    bias_t = bias_t.reshape(N_KV, HPG, N_BIAS_TILES, KB, QB).transpose(2, 0, 3, 1, 4).reshape(
        N_BIAS_TILES, N_KV, KB, HPG * QB)
    t_all = jnp.arange(seq, dtype=i32)
    c_end_p = jnp.arange(n_cmp_p, dtype=i32) * CMP_BLOCK + (CMP_BLOCK - 1)
    bias_c = _bias_from_dist(t_all[None, :] - c_end_p[:, None], rel_bias, LOG2E)
    bias_c = bias_c.reshape(N_KV, HPG, n_cmp_p, nq, QB).transpose(3, 0, 2, 1, 4).reshape(
        nq, N_KV, n_cmp_p, HPG * QB)

    cos_p, sin_p = _rope_tables(jnp.arange(seq, dtype=i32))
    dec_p = _retention_decays(RET_CHUNK)

    cols = HPG * N_KV * dseq
    t_cols = PAST_LEN + jnp.tile(jnp.arange(dseq, dtype=i32), HPG * N_KV)
    head_cols = jnp.tile(jnp.repeat(jnp.arange(N_KV, dtype=i32), dseq), HPG) * HPG + jnp.repeat(
        jnp.arange(HPG, dtype=i32), N_KV * dseq)
    t_new = PAST_LEN + jnp.arange(dseq, dtype=i32)

    def col_bias(key_pos):
        per_head = _bias_from_dist(t_new[:, None] - key_pos[None, :], rel_bias)
        t_idx = jnp.tile(jnp.arange(dseq, dtype=i32), HPG * N_KV)
        return per_head[head_cols, t_idx, :]

    n_cmp_s = PAST_LEN // CMP_BLOCK
    bias_c_s = col_bias(jnp.arange(n_cmp_s, dtype=i32) * CMP_BLOCK + (CMP_BLOCK - 1)).T
    bias_s_s = col_bias(jnp.arange(PAST_LEN, dtype=i32))
    bias_new_s = col_bias(PAST_LEN + jnp.arange(PAGE_SIZE, dtype=i32))
    w_buf = state_win_kv.shape[2]
    n_win_pad = -(-(w_buf + dseq) // LANES) * LANES
    bias_w_s = col_bias(PAST_LEN - w_buf + jnp.arange(n_win_pad, dtype=i32))
    n_sel_s = -(-(PAST_LEN + dseq) // SEL_BLOCK)
    n_sel_pad = -(-(n_pages * (PAGE_SIZE // SEL_BLOCK) + PAGE_SIZE // SEL_BLOCK) // SUBLANES) * SUBLANES
    assert n_sel_pad >= n_sel_s
    cos_s, sin_s = _rope_tables(t_new)
    cos_s, sin_s = jnp.tile(cos_s, (dbatch, 1)), jnp.tile(sin_s, (dbatch, 1))
    dec_s = _retention_decays(dseq)
    t_pos_row = t_cols.reshape(1, cols)
    t_pos_col = t_cols.reshape(cols, 1)

    page_major = lambda c: jnp.transpose(c, (0, 1, 3, 4, 5, 2)).reshape(depth, n_pool, PAIR_W, PAGE_SIZE)
    cache_cmp_t = page_major(cache_cmp_kv)
    cache_sel_t = page_major(cache_sel_kv)

    m_p = batch * seq
    m_s = dbatch * dseq
    tm_p = 1024
    tm_s = m_s

    xp = x_prompt.reshape(m_p, D_MODEL)
    xs = x_sample.reshape(m_s, D_MODEL)
    st_p, st_s = [], []
    for l in range(depth):
        w_attn, w_ng, w_ret = _prep_w_in(w_in[l])
        wk, wv, pos_row = _prep_w_phi(w_phi[l], cmp_pos[l])
        lw = dict(norm_mix=norm_mix[l], norm_mlp=norm_mlp[l], w_attn=w_attn, w_ng=w_ng, w_ret=w_ret,
                  w_nsa_proj=w_nsa_proj[l].astype(bf16), w_ret_proj=w_ret_proj[l].astype(bf16),
                  w_out=w_out[l].astype(bf16), w_up=w_up[l].astype(bf16), w_down=w_down[l].astype(bf16))
        gn_w = ret_gn_w[l].reshape(1, -1)
        gn_b = ret_gn_b[l].reshape(1, -1)

        mod_p = [a.reshape(batch, 1, D_MODEL) for a in jnp.split(mod_all[l, :batch], 6, axis=-1)]

        def attend_prompt(q, kv, gates, rqk, rv, gts):
            kv_raw, k_rows, vt_rows, raw_t = kv
            kc, vct = _compress_prompt(kv_raw, pos_row, wk, wv, batch)
            o = _nsa_prompt(q, kc, vct, k_rows, vt_rows, gates, bias_c, bias_t, batch, seq)
            zero_state = jnp.zeros((batch, RET_HEADS, RET_DK, RET_DV), f32)
            r, s_new = _retention(rqk, rv, gts, gn_w, gn_b, zero_state, *dec_p, batch, seq, RET_CHUNK, bf16)
            return o, r, (raw_t[0], raw_t[1], raw_t[2][:, :, seq - min(WINDOW, seq):], s_new)

        rope_p = (cos_p, sin_p, pl.BlockSpec((tm_p, RET_DK // 2), lambda i, j: (i % (seq // tm_p), 0)))
        xp, st = _layer(xp, mod_p, lw, seq, tm_p, bf16, HEAD_DIM ** -0.5 * LOG2E, True, rope_p, attend_prompt)
        st_p.append(st)

        mod_rows = jnp.repeat(mod_all[l, batch:batch + dbatch], dseq, axis=0)
        mod_s = jnp.split(mod_rows, 6, axis=-1)

        def attend_sample(q, kv_raw, gates, rqk, rv, gts):
            cmp_new, sel_new, win_new = (kv_raw[j].reshape(dbatch, dseq, PAIR_W) for j in range(3))
            kc, vct = _compress_sample(cache_cmp_t, l, page_table, pos_row, wk, wv)
            q5 = q.reshape(dbatch, dseq, N_KV, HPG, HEAD_DIM)
            qp = jnp.einsum('btghd,gk->bhgtkd', q5, jnp.eye(N_KV, dtype=q.dtype)).reshape(
                dbatch, cols, KV_W).astype(bf16)
            oc_t, sel_mask = _nsa_sample_cmp(kc, vct, jnp.swapaxes(qp, 1, 2), bias_c_s, t_pos_row,
                                             n_sel_s, n_sel_pad)
            g4 = gates[:, :N_KV * NG_SLOT].reshape(dbatch, dseq, N_KV, NG_SLOT)[..., :3 * HPG]
            g4 = g4.reshape(dbatch, dseq, N_KV, HPG, 3)
            gates_c = jnp.transpose(g4, (0, 3, 2, 1, 4)).reshape(dbatch, cols, 3)
            gates_c = jnp.pad(gates_c, ((0, 0), (0, 0), (0, LANES - 3)))
            sel_pad = jnp.pad(sel_new, ((0, 0), (0, PAGE_SIZE - dseq), (0, 0)))
            win_all = jnp.concatenate([state_win_kv[l].reshape(dbatch, w_buf, PAIR_W), win_new], axis=1)
            win_pad = jnp.pad(win_all, ((0, 0), (0, n_win_pad - w_buf - dseq), (0, 0)))
            o_q = _nsa_sample_sel(cache_sel_t, l, page_table, sel_pad, win_pad, qp, sel_mask, oc_t,
                                  gates_c, bias_s_s, bias_new_s, bias_w_s, t_pos_col)
            o6 = o_q.reshape(dbatch, HPG, N_KV, dseq, N_KV, HEAD_DIM)
            o = jnp.einsum('bhgtgd->btghd', o6).reshape(m_s, N_HEADS * HEAD_DIM)
            r, s_new = _retention(rqk, rv, gts, gn_w, gn_b, state_ret[l], *dec_s, dbatch, dseq, dseq, f32)
            kv6 = lambda a: a.reshape(dbatch, -1, 2, N_KV, HEAD_DIM)
            return o, r, (kv6(cmp_new), kv6(sel_new), kv6(win_all[:, -w_buf:]), s_new)

        rope_s = (cos_s, sin_s, pl.BlockSpec((tm_s, RET_DK // 2), lambda i, j: (i, 0)))
        xs, st = _layer(xs, mod_s, lw, dseq, tm_s, f32, HEAD_DIM ** -0.5, False, rope_s, attend_sample)
        st_s.append(st)

    y_p = _norm("norm_final", xp, norm_final, None, None, seq, f32).reshape(batch, seq, D_MODEL)
    y_s = _norm("norm_final", xs, norm_final, None, None, dseq, f32).reshape(dbatch, dseq, D_MODEL)
    cmp_p, sel_p, win_p, ret_p = [jnp.stack([st[i] for st in st_p]) for i in range(4)]
    rows_last = lambda a: jnp.transpose(a.reshape(depth, batch, 2, N_KV, HEAD_DIM, a.shape[-1]), (0, 1, 5, 2, 3, 4))
    cmp_p, sel_p, win_p = rows_last(cmp_p), rows_last(sel_p), rows_last(win_p)
    cmp_s, sel_s, win_s, ret_s = [jnp.stack([st[i] for st in st_s]) for i in range(4)]
    return (y_p, y_s, cmp_p, sel_p, win_p, ret_p, cmp_s, sel_s, win_s, ret_s)
```

```python
import functools
import math

import jax
import jax.numpy as jnp
from jax import lax
from jax.experimental import pallas as pl
from jax.experimental.pallas import tpu as pltpu

D_MODEL = 2048
DEPTH = 4
PAST_LEN = 16384
PAGE_SIZE = 128
N_HEADS = 16
HEAD_DIM = 64
N_KV = 4
HPG = N_HEADS // N_KV
CMP_BLOCK = 32
SEL_BLOCK = 64
TOP_K = 16
WINDOW = 512
RET_HEADS = 4
RET_DK = 256
RET_DV = 512
RET_CHUNK = 128
ROPE_BASE = 10000.0
D_FF = 4 * D_MODEL
N_BUCKETS = 32
REL_MAX_DIST = 2048
NORM_EPS = 1e-6
FORCE_BONUS = 1e6
NEG = -1e30

KV_W = N_KV * HEAD_DIM
PAIR_W = 2 * KV_W
NG_SLOT = 16
NG_W = 128

LANES = 128
SUBLANES = 8
VMEM_LIMIT_BYTES = 56 * 1024 * 1024

QB = 128
KB = 128
N_BIAS_TILES = 14
SEL_SUB = 8
LINEAR_TN = 1024
LINEAR_TK = 2048
LOG2E = math.log2(math.e)

f32 = jnp.float32
bf16 = jnp.bfloat16


def _cparams(sem):
    return pltpu.CompilerParams(dimension_semantics=sem, vmem_limit_bytes=VMEM_LIMIT_BYTES)


def _linear(name, xs, ws, extras, extra_specs, epilogue, out_shapes, out_specs, tm, tn, tk=None, n=None, w_col0=0):
    m = xs[0].shape[0]
    n = ws[0].shape[1] if n is None else n
    nx, ne = len(xs), len(extras)
    if tk is None:
        def body(*refs):
            x_refs, w_refs = refs[:nx], refs[nx:2 * nx]
            e_refs = refs[2 * nx:2 * nx + ne]
            o_refs = refs[2 * nx + ne:]
            accs = [jnp.dot(x[...].astype(bf16), w[...], preferred_element_type=f32)
                    for x, w in zip(x_refs, w_refs)]
            epilogue(accs, e_refs, o_refs)

        in_specs = ([pl.BlockSpec((tm, x.shape[1]), lambda i, j: (i, 0)) for x in xs]
                    + [pl.BlockSpec((w.shape[0], tn), lambda i, j: (0, j + w_col0)) for w in ws]
                    + list(extra_specs))
        return pl.pallas_call(
            body, grid=(m // tm, n // tn), in_specs=in_specs, out_specs=out_specs,
            out_shape=out_shapes, compiler_params=_cparams(("parallel", "arbitrary")),
            name=name)(*xs, *ws, *extras)

    assert nx == 1
    nk = xs[0].shape[1] // tk

    def body_k(x_ref, w_ref, *rest):
        e_refs, o_refs, acc_ref = rest[:ne], rest[ne:-1], rest[-1]
        k = pl.program_id(2)
        part = jnp.dot(x_ref[...].astype(bf16), w_ref[...], preferred_element_type=f32)

        @pl.when(k == 0)
        def _():
            acc_ref[...] = part

        @pl.when(k > 0)
        def _():
            acc_ref[...] += part

        @pl.when(k == nk - 1)
        def _():
            epilogue([acc_ref[...]], e_refs, o_refs)

    ij = lambda spec: pl.BlockSpec(spec.block_shape, lambda i, j, k, f=spec.index_map: f(i, j))
    single = not isinstance(out_specs, (list, tuple))
    outs = ij(out_specs) if single else [ij(s) for s in out_specs]
    in_specs = ([pl.BlockSpec((tm, tk), lambda i, j, k: (i, k)), pl.BlockSpec((tk, tn), lambda i, j, k: (k, j))]
                + [ij(s) for s in extra_specs])
    return pl.pallas_call(
        body_k, grid=(m // tm, n // tn, nk), in_specs=in_specs, out_specs=outs, out_shape=out_shapes,
        scratch_shapes=[pltpu.VMEM((tm, tn), f32)],
        compiler_params=_cparams(("parallel", "arbitrary", "arbitrary")), name=name)(*xs, *ws, *extras)


def _tile_spec(tm, tn, col0=0):
    return pl.BlockSpec((tm, tn), lambda i, j: (i, j + col0))


def _mod_spec(arr, tm, tn, rows_per_batch):
    if arr.ndim == 3:
        return pl.BlockSpec((None, 1, tn), lambda i, j: ((i * tm) // rows_per_batch, 0, j))
    return pl.BlockSpec((tm, tn), lambda i, j: (i, j))


def _mod_all(c_all, w_mod, b_mod):
    depth, d, n = w_mod.shape
    tn = 1024

    def body(c_ref, w_ref, b_ref, o_ref):
        c = c_ref[...]
        a = (c * jax.nn.sigmoid(c)).astype(bf16)
        o_ref[...] = jnp.dot(a, w_ref[...].astype(bf16), preferred_element_type=f32) + b_ref[...]

    return pl.pallas_call(
        body, grid=(depth, n // tn),
        in_specs=[pl.BlockSpec(c_all.shape, lambda l, j: (0, 0)),
                  pl.BlockSpec((None, d, tn), lambda l, j: (l, 0, j)),
                  pl.BlockSpec((None, 1, tn), lambda l, j: (l, 0, j))],
        out_specs=pl.BlockSpec((None, c_all.shape[0], tn), lambda l, j: (l, 0, j)),
        out_shape=jax.ShapeDtypeStruct((depth, c_all.shape[0], n), f32),
        compiler_params=_cparams(("parallel", "parallel")), name="mod_all",
    )(c_all, w_mod, b_mod.reshape(depth, 1, n))


def _norm(name, x, gain, sc, sh, rows_per_batch, out_dtype):
    m, d = x.shape
    tm = min(256, m)
    has_mod = sc is not None

    def body(*refs):
        if has_mod:
            x_ref, g_ref, sc_ref, sh_ref, o_ref = refs
        else:
            x_ref, g_ref, o_ref = refs
        xf = x_ref[...]
        y = xf * lax.rsqrt(jnp.mean(xf * xf, axis=-1, keepdims=True) + NORM_EPS) * g_ref[...]
        if has_mod:
            y = y * (1.0 + sc_ref[...]) + sh_ref[...]
        o_ref[...] = y.astype(out_dtype)

    in_specs = [pl.BlockSpec((tm, d), lambda i: (i, 0)), pl.BlockSpec((1, d), lambda i: (0, 0))]
    args = [x, gain.reshape(1, d)]
    if has_mod:
        for a in (sc, sh):
            if a.ndim == 3:
                in_specs.append(pl.BlockSpec((None, 1, d), lambda i: ((i * tm) // rows_per_batch, 0, 0)))
            else:
                in_specs.append(pl.BlockSpec((tm, d), lambda i: (i, 0)))
            args.append(a)
    return pl.pallas_call(
        body, grid=(m // tm,), in_specs=in_specs,
        out_specs=pl.BlockSpec((tm, d), lambda i: (i, 0)),
        out_shape=jax.ShapeDtypeStruct((m, d), out_dtype),
        compiler_params=_cparams(("parallel",)), name=name)(*args)


def _t5_bucket(dist):
    n = jnp.maximum(dist, 0)
    exact = N_BUCKETS // 2
    nf = jnp.maximum(n, exact).astype(f32)
    large = exact + (jnp.log(nf / exact) / math.log(REL_MAX_DIST / exact) * (N_BUCKETS - exact)).astype(jnp.int32)
    large = jnp.minimum(large, N_BUCKETS - 1)
    return jnp.where(n < exact, n, large)


def _bias_from_dist(dist, rel_bias, scale=1.0):
    r, c = dist.shape
    tr = next(t for t in (512, 256, 128, 64, 32, 16, 8) if r % t == 0)

    def body(tab_ref, d_ref, o_ref):
        h = pl.program_id(0)
        bucket = _t5_bucket(d_ref[...])
        acc = jnp.zeros(bucket.shape, f32)
        for k in range(N_BUCKETS):
            acc = jnp.where(bucket == k, tab_ref[k, h], acc)
        o_ref[...] = acc * scale

    return pl.pallas_call(
        body, grid=(N_HEADS, r // tr),
        in_specs=[pl.BlockSpec(memory_space=pltpu.SMEM),
                  pl.BlockSpec((tr, c), lambda h, i: (i, 0))],
        out_specs=pl.BlockSpec((None, tr, c), lambda h, i: (h, i, 0)),
        out_shape=jax.ShapeDtypeStruct((N_HEADS, r, c), f32),
        compiler_params=_cparams(("parallel", "parallel")), name="t5_bias",
    )(rel_bias, dist)


def _compress_accumulate(load_rows, pos_ref, wk_ref, wv_ref, rows):
    acc_k = jnp.zeros((rows, KV_W), f32)
    acc_v = jnp.zeros((rows, KV_W), f32)
    for c in range(CMP_BLOCK):
        lo = c * PAIR_W
        xk = (load_rows(c, 0) + pos_ref[:, lo:lo + KV_W]).astype(bf16)
        xv = (load_rows(c, 1) + pos_ref[:, lo + KV_W:lo + PAIR_W]).astype(bf16)
        acc_k = acc_k + jnp.dot(xk, wk_ref[c], preferred_element_type=f32)
        acc_v = acc_v + jnp.dot(xv, wv_ref[c], preferred_element_type=f32)
    return acc_k, acc_v


def _compress_prompt(kv_raw, pos_row, wk, wv, batch):
    seq = kv_raw.shape[1] // batch
    n = seq // CMP_BLOCK
    n_lane_tiles = PAIR_W // LANES

    def body(*refs):
        x_refs = refs[:n_lane_tiles]
        pos_ref, wk_ref, wv_ref, kc_ref, vct_ref = refs[n_lane_tiles:]

        def load_rows(c, kv):
            tiles = [x_refs[kv * (KV_W // LANES) + j][pl.ds(c, n, stride=CMP_BLOCK), :]
                     for j in range(KV_W // LANES)]
            return jnp.concatenate(tiles, axis=1)
        acc_k, acc_v = _compress_accumulate(load_rows, pos_ref, wk_ref, wv_ref, n)
        kc_ref[...] = acc_k.astype(bf16)
        vct_ref[...] = acc_v.T.astype(bf16)

    lane_tile = lambda j: pl.BlockSpec((None, seq, LANES), lambda b: (0, b, j))
    return pl.pallas_call(
        body, grid=(batch,),
        in_specs=[lane_tile(j) for j in range(n_lane_tiles)]
        + [pl.BlockSpec(pos_row.shape, lambda b: (0, 0)),
           pl.BlockSpec(wk.shape, lambda b: (0, 0, 0)),
           pl.BlockSpec(wv.shape, lambda b: (0, 0, 0))],
        out_specs=[pl.BlockSpec((None, n, KV_W), lambda b: (b, 0, 0)),
                   pl.BlockSpec((None, KV_W, n), lambda b: (b, 0, 0))],
        out_shape=[jax.ShapeDtypeStruct((batch, n, KV_W), bf16),
                   jax.ShapeDtypeStruct((batch, KV_W, n), bf16)],
        compiler_params=_cparams(("parallel",)), name="compress_prompt",
    )(*([kv_raw] * n_lane_tiles), pos_row, wk, wv)


CMP_PAGES_PER_STEP = 32


def _gather_pages(pt_ref, cache_ref, xbuf, sems, layer, pp, steps, n_steps):
    g = pl.program_id(0) * steps + pl.program_id(1)
    slot = g % 2

    def copies(step, into):
        row, first = step // steps, (step % steps) * pp
        return [pltpu.make_async_copy(cache_ref.at[layer, pt_ref[row, first + p]], xbuf.at[into, p],
                                      sems.at[into, p]) for p in range(pp)]

    @pl.when(g == 0)
    def _():
        for c in copies(g, slot):
            c.start()

    @pl.when(g + 1 < n_steps)
    def _():
        for c in copies(g + 1, 1 - slot):
            c.start()

    for c in copies(g, slot):
        c.wait()
    return slot


CMP_ROW_PITCH = CMP_BLOCK + SUBLANES


def _compress_sample(cache_t, layer, page_table, pos_row, wk, wv):
    batch, n_pages = page_table.shape
    blocks_per_page = PAGE_SIZE // CMP_BLOCK
    pp = CMP_PAGES_PER_STEP
    steps = n_pages // pp
    rows = pp * blocks_per_page
    n_lane_tiles = PAIR_W // LANES
    page_pitch = blocks_per_page * CMP_ROW_PITCH

    def body(pt_ref, cache_ref, pos_ref, wk_ref, wv_ref, kc_ref, vct_ref, xbuf, sems, tscr):
        slot = _gather_pages(pt_ref, cache_ref, xbuf, sems, layer, pp, steps, batch * steps)

        def untranspose(p, carry):
            base = pl.multiple_of(p * page_pitch, SUBLANES)
            for j in range(n_lane_tiles):
                tile = xbuf[slot, p, j * LANES:(j + 1) * LANES, :].T
                for n in range(blocks_per_page):
                    tscr[j, pl.ds(base + n * CMP_ROW_PITCH, CMP_BLOCK), :] = tile[n * CMP_BLOCK:(n + 1) * CMP_BLOCK]
            return carry
        lax.fori_loop(0, pp, untranspose, 0, unroll=4)

        def load_rows(c, kv):
            tiles = [tscr[kv * (KV_W // LANES) + j, pl.ds(c, rows, stride=CMP_ROW_PITCH), :]
                     for j in range(KV_W // LANES)]
            return jnp.concatenate(tiles, axis=1)
        acc_k, acc_v = _compress_accumulate(load_rows, pos_ref, wk_ref, wv_ref, rows)
        kc_ref[...] = acc_k.astype(bf16)
        vct_ref[...] = acc_v.T.astype(bf16)

    grid_spec = pltpu.PrefetchScalarGridSpec(
        num_scalar_prefetch=1, grid=(batch, steps),
        in_specs=[pl.BlockSpec(memory_space=pl.ANY),
                  pl.BlockSpec(pos_row.shape, lambda b, s, pt: (0, 0)),
                  pl.BlockSpec(wk.shape, lambda b, s, pt: (0, 0, 0)),
                  pl.BlockSpec(wv.shape, lambda b, s, pt: (0, 0, 0))],
        out_specs=[pl.BlockSpec((None, rows, KV_W), lambda b, s, pt: (b, s, 0)),
                   pl.BlockSpec((None, KV_W, rows), lambda b, s, pt: (b, 0, s))],
        scratch_shapes=[pltpu.VMEM((2, pp, PAIR_W, PAGE_SIZE), f32), pltpu.SemaphoreType.DMA((2, pp)),
                        pltpu.VMEM((n_lane_tiles, pp * page_pitch, LANES), f32)])
    n_cmp = n_pages * blocks_per_page
    return pl.pallas_call(
        body, grid_spec=grid_spec,
        out_shape=[jax.ShapeDtypeStruct((batch, n_cmp, KV_W), bf16),
                   jax.ShapeDtypeStruct((batch, KV_W, n_cmp), bf16)],
        compiler_params=_cparams(("arbitrary", "arbitrary")), name="compress_sample",
    )(page_table, cache_t, pos_row, wk, wv)


def _softmax_cols(s, mask, exp=jnp.exp):
    sm = jnp.where(mask, s, NEG)
    m = jnp.max(sm, axis=0, keepdims=True)
    e = exp(sm - m)
    p = e / jnp.sum(e, axis=0, keepdims=True)
    return jnp.where(mask, p, 0.0)


def _topk_mask(score, score_scr, n_rows, k, groups):
    n_pad, c = score.shape
    per = -(-n_rows // groups)
    assert groups * per <= n_pad
    rows = lax.broadcasted_iota(jnp.int32, score.shape, 0)
    lane_grp = lax.broadcasted_iota(jnp.int32, (1, c), 1) // (c // groups)
    score_scr[...] = score

    def step(j, cnt):
        sj = score_scr[pl.ds(j, 1), :]
        for h in range(1, groups):
            sj = jnp.where(lane_grp == h, score_scr[pl.ds(j + h * per, 1), :], sj)
        jv = j + per * lane_grp
        ahead = (sj > score) | ((sj == score) & (jv < rows))
        return cnt + jnp.where(ahead, 1.0, 0.0)

    cnt = lax.fori_loop(0, per, step, jnp.zeros(score.shape, f32))
    total = cnt
    for h in range(1, groups):
        total = total + pltpu.roll(cnt, h * (c // groups), axis=1)
    return jnp.where(total < k, 1.0, 0.0)


def _topk_mask_unrolled(score, score_scr, k):
    n_rows, c = score.shape
    score_scr[...] = score
    n_tiles = n_rows // SUBLANES
    tiles = [score[SUBLANES * v:SUBLANES * (v + 1)] for v in range(n_tiles)]
    row_in_tile = lax.broadcasted_iota(jnp.int32, (SUBLANES, c), 0)
    cnt = [jnp.zeros((SUBLANES, c), f32) for _ in range(n_tiles)]
    for j in range(n_rows):
        sj = score_scr[j:j + 1, :]
        vj = j // SUBLANES
        for v in range(n_tiles):
            if v > vj:
                ahead = sj >= tiles[v]
            elif v < vj:
                ahead = sj > tiles[v]
            else:
                ahead = (sj > tiles[v]) | ((sj == tiles[v]) & (row_in_tile > j % SUBLANES))
            cnt[v] = cnt[v] + jnp.where(ahead, 1.0, 0.0)
    return jnp.concatenate([jnp.where(cv < k, 1.0, 0.0) for cv in cnt], axis=0)


def _nsa_prompt(q, kc, vct, k_rows, vt_rows, gates, bias_c, bias_t, batch, seq):
    nq = seq // QB
    n_cmp = seq // CMP_BLOCK
    n_sel = seq // SEL_BLOCK
    ratio = SEL_BLOCK // CMP_BLOCK
    k_eff = min(TOP_K, n_sel)
    wq = HPG * QB
    win_chunks = WINDOW // KB

    def body(q_ref, kc_ref, vct_ref, ks_ref, vst_ref, kw_ref, vwt_ref, g_ref, bc_ref, bt_ref,
             o_ref, imp_scr, score_scr, sel_scr, gate_scr):
        g = pl.program_id(1)
        i = pl.program_id(2)
        q0 = i * QB

        qt = q_ref[...].astype(f32).T
        qcat = jnp.concatenate([qt[h * HEAD_DIM:(h + 1) * HEAD_DIM] for h in range(HPG)], axis=1)
        qrep = jnp.concatenate([qcat] * N_KV, axis=0)
        row_grp = lax.broadcasted_iota(jnp.int32, (KV_W, wq), 0) // HEAD_DIM
        qpad = jnp.where(row_grp == g, qrep, 0.0).astype(bf16)

        t_row = q0 + lax.broadcasted_iota(jnp.int32, (1, wq), 1) % QB

        kj = lax.broadcasted_iota(jnp.int32, (KB, QB), 0)
        qi = lax.broadcasted_iota(jnp.int32, (KB, QB), 1)
        rel = qi - kj
        upper = kj < SEL_BLOCK

        def scores(k_ref, chunk, bias, mask):
            k0 = pl.multiple_of(chunk * KB, KB)
            s = jnp.dot(k_ref[pl.ds(k0, KB), :], qpad, preferred_element_type=f32)
            penalty = jnp.where(mask, 0.0, NEG)
            return s + bias + jnp.concatenate([penalty] * HPG, axis=1)

        def weighted_values(vt_ref, chunks, ps):
            return sum(jnp.dot(vt_ref[:, pl.ds(pl.multiple_of(c * KB, KB), KB)], p.astype(bf16),
                               preferred_element_type=f32) for c, p in zip(chunks, ps))

        def col_max(ss):
            return functools.reduce(jnp.maximum, [jnp.max(s, axis=0, keepdims=True) for s in ss])

        chunks, ss = [], []
        for dd in range(win_chunks + 1):
            c = i - dd
            dist = rel + (dd * KB + jnp.where(c >= 0, 0, 2 * WINDOW))
            mask = (dist >= 0) & (dist < WINDOW)
            chunks.append(jnp.maximum(c, 0))
            ss.append(scores(kw_ref, chunks[-1], bt_ref[dd], mask))
        m_w = col_max(ss)
        ps = [jnp.exp2(s - m_w) for s in ss]
        l_w = sum(jnp.sum(p, axis=0, keepdims=True) for p in ps)
        o_w = weighted_values(vwt_ref, chunks, ps) / l_w

        s_c = jnp.dot(kc_ref[...], qpad, preferred_element_type=f32) + bc_ref[...]
        c_end = lax.broadcasted_iota(jnp.int32, (n_cmp, wq), 0) * CMP_BLOCK + (CMP_BLOCK - 1)
        p_c = _softmax_cols(s_c, c_end <= t_row, jnp.exp2)
        o_c = jnp.dot(vct_ref[...], p_c.astype(bf16), preferred_element_type=f32)

        imp_scr[...] = sum(p_c[:, h * QB:(h + 1) * QB] for h in range(HPG))
        imp = sum(imp_scr[pl.ds(r, n_sel, stride=ratio), :] for r in range(ratio))
        blk = lax.broadcasted_iota(jnp.int32, (n_sel, QB), 0)
        t_q = q0 + lax.broadcasted_iota(jnp.int32, (n_sel, QB), 1)
        cur = t_q // SEL_BLOCK
        forced = (blk == 0) | (blk == cur) | (blk == cur - 1)
        score = imp + jnp.where(forced, FORCE_BONUS, 0.0)
        score = jnp.where(blk * SEL_BLOCK <= t_q, score, NEG)
        sel_scr[...] = _topk_mask_unrolled(score, score_scr, k_eff)

        def sel_chunks(j):
            return [jnp.minimum(j * SEL_SUB + r, i) for r in range(SEL_SUB)]

        def sel_scores(j):
            ss = []
            for r, cc in enumerate(sel_chunks(j)):
                rows2 = sel_scr[pl.ds(pl.multiple_of(cc * (KB // SEL_BLOCK), KB // SEL_BLOCK),
                                      KB // SEL_BLOCK), :]
                chosen = jnp.where(upper, rows2[0:1, :], rows2[1:2, :]) > 0.5
                mask = chosen & (rel + (i - (j * SEL_SUB + r)) * KB >= 0)
                ss.append(scores(ks_ref, cc, bt_ref[jnp.minimum(i - cc, N_BIAS_TILES - 1)], mask))
            return ss

        def sel_step(j, carry):
            m, l, acc = carry
            ss = sel_scores(j)
            m_new = jnp.maximum(m, col_max(ss))
            alpha = jnp.exp2(m - m_new)
            ps = [jnp.exp2(s - m_new) for s in ss]
            l = alpha * l + sum(jnp.sum(p, axis=0, keepdims=True) for p in ps)
            acc = alpha * acc + weighted_values(vst_ref, sel_chunks(j), ps)
            return m_new, l, acc

        init = (jnp.full((1, wq), NEG, f32), jnp.zeros((1, wq), f32), jnp.zeros((HEAD_DIM, wq), f32))
        _, l_s, acc_s = lax.fori_loop(0, (i + SEL_SUB) // SEL_SUB, sel_step, init)
        o_s = acc_s / l_s

        gate_scr[...] = g_ref[...].T
        gg = gate_scr[pl.ds(pl.multiple_of(g * NG_SLOT, NG_SLOT), NG_SLOT), :]
        outs = []
        for h in range(HPG):
            sl = slice(h * QB, (h + 1) * QB)
            outs.append(gg[3 * h:3 * h + 1] * o_c[:, sl] + gg[3 * h + 1:3 * h + 2] * o_s[:, sl]
                        + gg[3 * h + 2:3 * h + 3] * o_w[:, sl])
        o_ref[...] = jnp.concatenate(outs, axis=0).T.astype(o_ref.dtype)

    return pl.pallas_call(
        body, grid=(batch, N_KV, nq),
        in_specs=[
            pl.BlockSpec((QB, KV_W), lambda b, g, i: (b * nq + i, g)),
            pl.BlockSpec((None, n_cmp, KV_W), lambda b, g, i: (b, 0, 0)),
            pl.BlockSpec((None, HEAD_DIM, n_cmp), lambda b, g, i: (b, g, 0)),
            pl.BlockSpec((None, seq, KV_W), lambda b, g, i: (1, b, 0)),
            pl.BlockSpec((None, None, HEAD_DIM, seq), lambda b, g, i: (1, b, g, 0)),
            pl.BlockSpec((None, seq, KV_W), lambda b, g, i: (2, b, 0)),
            pl.BlockSpec((None, None, HEAD_DIM, seq), lambda b, g, i: (2, b, g, 0)),
            pl.BlockSpec((QB, NG_W), lambda b, g, i: (b * nq + i, 0)),
            pl.BlockSpec((None, None, n_cmp, wq), lambda b, g, i: (i, g, 0, 0)),
            pl.BlockSpec((N_BIAS_TILES, None, KB, wq), lambda b, g, i: (0, g, 0, 0)),
        ],
        out_specs=pl.BlockSpec((QB, KV_W), lambda b, g, i: (b * nq + i, g)),
        out_shape=jax.ShapeDtypeStruct((batch * seq, N_HEADS * HEAD_DIM), bf16),
        scratch_shapes=[pltpu.VMEM((n_cmp, QB), f32), pltpu.VMEM((n_sel, QB), f32),
                        pltpu.VMEM((n_sel, QB), f32), pltpu.VMEM((NG_W, QB), f32)],
        compiler_params=_cparams(("parallel", "parallel", "arbitrary")), name="nsa_prompt",
    )(q, kc, vct, k_rows, vt_rows, k_rows, vt_rows, gates, bias_c, bias_t)


def _retention(qk, v, rg, gn_w, gn_b, state0, dec_in, dec_q, dec_k, dec_c, batch, length, chunk, out_dtype):
    n = length // chunk
    w_qk, w_v = RET_HEADS * RET_DK, RET_HEADS * RET_DV

    def body(q_ref, k_ref, v_ref, rg_ref, gw_ref, gb_ref, s0_ref, din_ref, dq_ref, dk_ref, dc_ref,
             r_ref, s_ref):
        @pl.when(pl.program_id(1) == 0)
        def _():
            s_ref[...] = s0_ref[...]

        for h in range(RET_HEADS):
            qs, vs = slice(h * RET_DK, (h + 1) * RET_DK), slice(h * RET_DV, (h + 1) * RET_DV)
            qc = q_ref[:, qs].astype(bf16)
            kf = k_ref[:, qs].astype(f32)
            vc = v_ref[:, vs].astype(bf16)
            st = s_ref[h]
            inner = lax.dot_general(qc, kf.astype(bf16), (((1,), (1,)), ((), ())),
                                    preferred_element_type=f32) * din_ref[h]
            o = jnp.dot(inner.astype(bf16), vc, preferred_element_type=f32)
            o = o + jnp.dot(qc, st.astype(bf16), preferred_element_type=f32) * dq_ref[h]
            kd = (kf * dk_ref[h]).astype(bf16)
            s_ref[h] = st * dc_ref[h] + lax.dot_general(kd, vc, (((0,), (0,)), ((), ())),
                                                        preferred_element_type=f32)
            mu = jnp.mean(o, axis=-1, keepdims=True)
            var = jnp.mean(jnp.square(o - mu), axis=-1, keepdims=True)
            y = (o - mu) * lax.rsqrt(var + NORM_EPS) * gw_ref[:, vs] + gb_ref[:, vs]
            r_ref[:, vs] = (rg_ref[:, vs].astype(f32) * y).astype(r_ref.dtype)

    seq_spec = lambda w, col0=0: pl.BlockSpec((chunk, w), lambda b, c: (b * n + c, col0))
    full_spec = lambda a: pl.BlockSpec(a.shape, lambda b, c: (0,) * a.ndim)
    state_spec = pl.BlockSpec((None, RET_HEADS, RET_DK, RET_DV), lambda b, c: (b, 0, 0, 0))
    return pl.pallas_call(
        body, grid=(batch, n),
        in_specs=[seq_spec(w_qk), seq_spec(w_qk, 1), seq_spec(w_v), seq_spec(w_v),
                  full_spec(gn_w), full_spec(gn_b), state_spec,
                  full_spec(dec_in), full_spec(dec_q), full_spec(dec_k), full_spec(dec_c)],
        out_specs=[seq_spec(w_v), state_spec],
        out_shape=[jax.ShapeDtypeStruct((batch * length, w_v), out_dtype),
                   jax.ShapeDtypeStruct((batch, RET_HEADS, RET_DK, RET_DV), f32)],
        compiler_params=_cparams(("parallel", "arbitrary")), name="retention",
    )(qk, qk, v, rg, gn_w, gn_b, state0, dec_in, dec_q, dec_k, dec_c)


def _retention_decays(chunk):
    lg = jnp.log(1.0 - jnp.exp2(-5.0 - jnp.arange(RET_HEADS, dtype=f32)))
    i = jnp.arange(chunk, dtype=f32)
    diff = i[:, None] - i[None, :]
    dec_in = jnp.where(diff >= 0, jnp.exp(jnp.maximum(diff, 0.0)[None] * lg[:, None, None]), 0.0)
    dec_q = jnp.exp((i[None, :] + 1.0) * lg[:, None])[..., None]
    dec_k = jnp.exp((chunk - 1.0 - i)[None, :] * lg[:, None])[..., None]
    dec_c = jnp.exp(chunk * lg)[:, None, None]
    return dec_in, dec_q, dec_k, dec_c


def _rope_tables(pos):
    half = RET_DK // 2
    inv = ROPE_BASE ** (-jnp.arange(half, dtype=f32) / half)
    ang = pos.astype(f32)[:, None] * inv[None, :]
    return jnp.cos(ang), jnp.sin(ang)


def _nsa_sample_cmp(kc, vct, qpt, bias_c, t_pos_row, n_sel, n_sel_pad):
    batch, n_cmp, _ = kc.shape
    ratio = SEL_BLOCK // CMP_BLOCK
    cols = qpt.shape[2]
    grp_cols = cols // HPG

    def body(kc_ref, vct_ref, q_ref, bc_ref, t_ref, oc_ref, sel_ref, imp_scr, score_scr):
        t_row = t_ref[...]
        s_c = jnp.dot(kc_ref[...], q_ref[...], preferred_element_type=f32) + bc_ref[...]
        c_end = lax.broadcasted_iota(jnp.int32, (n_cmp, cols), 0) * CMP_BLOCK + (CMP_BLOCK - 1)
        p_c = _softmax_cols(s_c, c_end <= t_row)
        oc_ref[...] = jnp.dot(vct_ref[...], p_c.astype(bf16), preferred_element_type=f32)
        assert HPG == 4
        pair = p_c + pltpu.roll(p_c, 2 * grp_cols, axis=1)
        imp_rep = pair + pltpu.roll(pair, grp_cols, axis=1)
        imp_scr[...] = jnp.zeros(imp_scr.shape, f32)
        imp_scr[pl.ds(0, n_cmp), :] = imp_rep
        imp = sum(imp_scr[pl.ds(r, n_sel_pad, stride=ratio), :] for r in range(ratio))
        blk = lax.broadcasted_iota(jnp.int32, (n_sel_pad, cols), 0)
        cur = t_row // SEL_BLOCK
        forced = (blk == 0) | (blk == cur) | (blk == cur - 1)
        score = imp + jnp.where(forced, FORCE_BONUS, 0.0)
        score = jnp.where(blk * SEL_BLOCK <= t_row, score, NEG)
        score = jnp.where(blk < n_sel, score, -jnp.inf)
        sel_ref[...] = _topk_mask(score, score_scr, n_sel, min(TOP_K, n_sel), HPG)

    return pl.pallas_call(
        body, grid=(batch,),
        in_specs=[pl.BlockSpec((None, n_cmp, KV_W), lambda b: (b, 0, 0)),
                  pl.BlockSpec((None, KV_W, n_cmp), lambda b: (b, 0, 0)),
                  pl.BlockSpec((None, KV_W, cols), lambda b: (b, 0, 0)),
                  pl.BlockSpec((n_cmp, cols), lambda b: (0, 0)),
                  pl.BlockSpec((1, cols), lambda b: (0, 0))],
        out_specs=[pl.BlockSpec((None, KV_W, cols), lambda b: (b, 0, 0)),
                   pl.BlockSpec((None, n_sel_pad, cols), lambda b: (b, 0, 0))],
        out_shape=[jax.ShapeDtypeStruct((batch, KV_W, cols), f32),
                   jax.ShapeDtypeStruct((batch, n_sel_pad, cols), f32)],
        scratch_shapes=[pltpu.VMEM((n_sel_pad * ratio, cols), f32), pltpu.VMEM((n_sel_pad, cols), f32)],
        compiler_params=_cparams(("parallel",)), name="nsa_sample_cmp",
    )(kc, vct, qpt, bias_c, t_pos_row)


SEL_PAGES_PER_STEP = 16


def _softmax_rows(s, mask):
    sm = jnp.where(mask, s, NEG)
    m = jnp.max(sm, axis=1, keepdims=True)
    e = jnp.exp(sm - m)
    p = e / jnp.sum(e, axis=1, keepdims=True)
    return jnp.where(mask, p, 0.0)


def _nsa_sample_sel(cache_t, layer, page_table, sel_new, win_all, qp, sel_mask, oc_t, gates_c,
                    bias_s, bias_new, bias_w, t_pos_col):
    batch, n_pages = page_table.shape
    pp = SEL_PAGES_PER_STEP
    steps = n_pages // pp
    keys = pp * PAGE_SIZE
    nq = qp.shape[1]
    blocks_per_page = PAGE_SIZE // SEL_BLOCK
    nb = pp * blocks_per_page
    n_win_pad = win_all.shape[1]
    w_buf = min(WINDOW, PAST_LEN)
    nt = (((1,), (1,)), ((), ()))
    tn = (((0,), (0,)), ((), ()))

    def body(pt_ref, cache_ref, new_ref, win_ref, q_ref, selm_ref, oc_ref, g_ref, bs_ref, bn_ref, bw_ref,
             t_ref, o_ref, xbuf, sems, m_scr, l_scr, acc_scr):
        s = pl.program_id(1)
        slot = _gather_pages(pt_ref, cache_ref, xbuf, sems, layer, pp, steps, batch * steps)

        @pl.when(s == 0)
        def _():
            m_scr[...] = jnp.full(m_scr.shape, NEG, f32)
            l_scr[...] = jnp.zeros(l_scr.shape, f32)
            acc_scr[...] = jnp.zeros(acc_scr.shape, f32)

        q = q_ref[...]
        t_col = t_ref[...]

        def chosen_keys(block_rows, n_keys):
            n_blocks = block_rows.shape[0]
            key_blk = lax.broadcasted_iota(jnp.int32, (n_blocks, n_keys), 1) // SEL_BLOCK
            spread = jnp.where(key_blk == lax.broadcasted_iota(jnp.int32, (n_blocks, n_keys), 0), 1.0, 0.0)
            return lax.dot_general(block_rows, spread, tn, preferred_element_type=f32) > 0.5

        def online(s_qk, mask, weighted_values):
            s_m = jnp.where(mask, s_qk, NEG)
            m_old = m_scr[...]
            m_new = jnp.maximum(m_old, jnp.max(s_m, axis=1, keepdims=True))
            alpha = jnp.exp(m_old - m_new)
            p = jnp.where(mask, jnp.exp(s_m - m_new), 0.0)
            l_scr[...] = alpha * l_scr[...] + jnp.sum(p, axis=1, keepdims=True)
            acc_scr[...] = alpha * acc_scr[...] + weighted_values(p.astype(bf16))
            m_scr[...] = m_new

        s_qk = jnp.concatenate(
            [jnp.dot(q, xbuf[slot, p, :KV_W, :].astype(bf16), preferred_element_type=f32) for p in range(pp)],
            axis=1) + bs_ref[...]
        mask = chosen_keys(selm_ref[pl.ds(pl.multiple_of(s * nb, nb), nb), :], keys)
        online(s_qk, mask, lambda pb: sum(
            lax.dot_general(pb[:, p * PAGE_SIZE:(p + 1) * PAGE_SIZE], xbuf[slot, p, KV_W:, :].astype(bf16), nt,
                            preferred_element_type=f32) for p in range(pp)))

        @pl.when(s == steps - 1)
        def _():
            kv_n = new_ref[...]
            s_n = lax.dot_general(q, kv_n[:, :KV_W].astype(bf16), nt, preferred_element_type=f32) + bn_ref[...]
            pos_n = PAST_LEN + lax.broadcasted_iota(jnp.int32, (nq, PAGE_SIZE), 1)
            mask_n = chosen_keys(selm_ref[pl.ds(n_pages * blocks_per_page, SUBLANES), :], PAGE_SIZE) & (pos_n <= t_col)
            online(s_n, mask_n, lambda pb: jnp.dot(pb, kv_n[:, KV_W:].astype(bf16), preferred_element_type=f32))
            o_s = acc_scr[...] / l_scr[...]

            kv_w = win_ref[...]
            kpos = (PAST_LEN - w_buf) + lax.broadcasted_iota(jnp.int32, (nq, n_win_pad), 1)
            dist = t_col - kpos
            mask_w = (dist >= 0) & (dist < WINDOW) & (kpos >= 0)
            s_w = lax.dot_general(q, kv_w[:, :KV_W].astype(bf16), nt, preferred_element_type=f32) + bw_ref[...]
            p_w = _softmax_rows(s_w, mask_w)
            o_w = jnp.dot(p_w.astype(bf16), kv_w[:, KV_W:].astype(bf16), preferred_element_type=f32)
            gts = g_ref[...]
            o_ref[...] = gts[:, 0:1] * oc_ref[...].T + gts[:, 1:2] * o_s + gts[:, 2:3] * o_w

    grid_spec = pltpu.PrefetchScalarGridSpec(
        num_scalar_prefetch=1, grid=(batch, steps),
        in_specs=[pl.BlockSpec(memory_space=pl.ANY),
                  pl.BlockSpec((None, PAGE_SIZE, PAIR_W), lambda b, s, pt: (b, 0, 0)),
                  pl.BlockSpec((None, n_win_pad, PAIR_W), lambda b, s, pt: (b, 0, 0)),
                  pl.BlockSpec((None, nq, KV_W), lambda b, s, pt: (b, 0, 0)),
                  pl.BlockSpec((None, sel_mask.shape[1], nq), lambda b, s, pt: (b, 0, 0)),
                  pl.BlockSpec((None, KV_W, nq), lambda b, s, pt: (b, 0, 0)),
                  pl.BlockSpec((None, nq, LANES), lambda b, s, pt: (b, 0, 0)),
                  pl.BlockSpec((nq, keys), lambda b, s, pt: (0, s)),
                  pl.BlockSpec((nq, PAGE_SIZE), lambda b, s, pt: (0, 0)),
                  pl.BlockSpec((nq, n_win_pad), lambda b, s, pt: (0, 0)),
                  pl.BlockSpec((nq, 1), lambda b, s, pt: (0, 0))],
        out_specs=pl.BlockSpec((None, nq, KV_W), lambda b, s, pt: (b, 0, 0)),
        scratch_shapes=[pltpu.VMEM((2, pp, PAIR_W, PAGE_SIZE), f32), pltpu.SemaphoreType.DMA((2, pp)),
                        pltpu.VMEM((nq, 1), f32), pltpu.VMEM((nq, 1), f32),
                        pltpu.VMEM((nq, KV_W), f32)])
    return pl.pallas_call(
        body, grid_spec=grid_spec,
        out_shape=jax.ShapeDtypeStruct((batch, nq, KV_W), f32),
        compiler_params=_cparams(("arbitrary", "arbitrary")), name="nsa_sample_sel",
    )(page_table, cache_t, sel_new, win_all, qp, sel_mask, oc_t, gates_c, bias_s, bias_new, bias_w,
      t_pos_col)


_OFF_Q = 0
_OFF_KV = N_HEADS * HEAD_DIM
_OFF_NG = _OFF_KV + 6 * KV_W
_OFF_RQ = _OFF_NG + 3 * N_HEADS
_OFF_RV = _OFF_RQ + 2 * RET_HEADS * RET_DK
_OFF_RG = _OFF_RV + RET_HEADS * RET_DV
_D_IN = _OFF_RG + RET_HEADS * RET_DV + 2 * D_MODEL


def _prep_w_in(w_in_l):
    w_attn = w_in_l[:, :_OFF_NG].astype(bf16)
    ng = w_in_l[:, _OFF_NG:_OFF_RQ].astype(bf16).reshape(D_MODEL, N_KV, 3 * HPG)
    w_ng = jnp.pad(ng, ((0, 0), (0, 0), (0, NG_SLOT - 3 * HPG))).reshape(D_MODEL, N_KV * NG_SLOT)
    w_ng = jnp.pad(w_ng, ((0, 0), (0, NG_W - N_KV * NG_SLOT)))
    w_ret = w_in_l[:, _OFF_RQ:].astype(bf16)
    return w_attn, w_ng, w_ret


def _prep_w_phi(w_phi_l, cmp_pos_l):
    eye = jnp.eye(N_KV, dtype=f32)
    wk = jnp.einsum('gh,cde->cgdhe', eye, w_phi_l[0]).reshape(CMP_BLOCK, KV_W, KV_W).astype(bf16)
    wv = jnp.einsum('gh,cde->cgdhe', eye, w_phi_l[1]).reshape(CMP_BLOCK, KV_W, KV_W).astype(bf16)
    pos = jnp.broadcast_to(cmp_pos_l[:, :, None, :], (2, CMP_BLOCK, N_KV, HEAD_DIM))
    pos_row = jnp.transpose(pos, (1, 0, 2, 3)).reshape(1, CMP_BLOCK * PAIR_W)
    return wk, wv, pos_row


def _layer(x, mod, lw, rows_per_batch, tm, act, q_scale, kv_layouts, rope, attend):
    m = x.shape[0]
    sh1, sc1, g1, sh2, sc2, g2 = mod
    h = _norm("norm_mix", x, lw["norm_mix"], sc1, sh1, rows_per_batch, act)

    def plain(dtype, scale=1.0):
        def epi(accs, e_refs, o_refs):
            o_refs[0][...] = (accs[0] * scale).astype(dtype)
        return epi

    w_attn, w_ng, w_ret = lw["w_attn"], lw["w_ng"], lw["w_ret"]
    tn = LINEAR_TN
    q = _linear("proj_q", [h], [w_attn], [], [], plain(act, q_scale),
                jax.ShapeDtypeStruct((m, _OFF_KV), act), _tile_spec(tm, tn), tm, tn, n=_OFF_KV)
    kv_cols = dict(n=3 * PAIR_W, w_col0=_OFF_KV // PAIR_W)
    pair_spec = lambda w: pl.BlockSpec((None, tm, w), lambda i, j: (j, i, 0))
    if kv_layouts:
        n_b, per_b = m // rows_per_batch, rows_per_batch // tm
        t_spec = lambda w: pl.BlockSpec((None, None, w, tm), lambda i, j: (j, i // per_b, 0, i % per_b))

        def kv_epi(accs, e_refs, o_refs):
            a = accs[0]
            a_t = a.T
            o_refs[0][...] = a
            o_refs[1][...] = a[:, :KV_W].astype(bf16)
            o_refs[2][...] = a_t[KV_W:].astype(bf16)
            o_refs[3][...] = a_t
        kv_raw = _linear("proj_kv", [h], [w_attn], [], [], kv_epi,
                         [jax.ShapeDtypeStruct((3, m, PAIR_W), f32), jax.ShapeDtypeStruct((3, m, KV_W), bf16),
                          jax.ShapeDtypeStruct((3, n_b, KV_W, rows_per_batch), bf16),
                          jax.ShapeDtypeStruct((3, n_b, PAIR_W, rows_per_batch), f32)],
                         [pair_spec(PAIR_W), pair_spec(KV_W), t_spec(KV_W), t_spec(PAIR_W)], tm, PAIR_W,
                         **kv_cols)
    else:
        kv_raw = _linear("proj_kv", [h], [w_attn], [], [], plain(f32),
                         jax.ShapeDtypeStruct((3, m, PAIR_W), f32), pair_spec(PAIR_W), tm, PAIR_W, **kv_cols)

    def sig_epi(accs, e_refs, o_refs):
        o_refs[0][...] = jax.nn.sigmoid(accs[0])
    gates = _linear("proj_ng", [h], [w_ng], [], [], sig_epi,
                    jax.ShapeDtypeStruct((m, NG_W), f32), _tile_spec(tm, NG_W), tm, NG_W)

    cos, sin, rope_spec = rope
    n_qk_tiles = RET_HEADS * RET_DK // tn

    def rot_epi(accs, e_refs, o_refs):
        a = accs[0]
        c, s = e_refs[0][...], e_refs[1][...]
        scale = jnp.where(pl.program_id(1) >= n_qk_tiles, RET_DK ** -0.5, 1.0)
        half = RET_DK // 2
        outs = []
        for hh in range(tn // RET_DK):
            x1 = a[:, hh * RET_DK:hh * RET_DK + half]
            x2 = a[:, hh * RET_DK + half:(hh + 1) * RET_DK]
            outs += [x1 * c - x2 * s, x2 * c + x1 * s]
        o_refs[0][...] = (jnp.concatenate(outs, axis=1) * scale).astype(act)

    n_rot = 2 * RET_HEADS * RET_DK
    rqk = _linear("proj_rqk", [h], [w_ret], [cos, sin], [rope_spec, rope_spec], rot_epi,
                  jax.ShapeDtypeStruct((m, n_rot), act), _tile_spec(tm, tn), tm, tn, n=n_rot)
    n_rv = RET_HEADS * RET_DV
    rv = _linear("proj_rv", [h], [w_ret], [], [], plain(act),
                 jax.ShapeDtypeStruct((m, n_rv), act), _tile_spec(tm, tn), tm, tn, n=n_rv, w_col0=n_rot // tn)
    n_silu_tiles = n_rv // tn

    def gates_epi(accs, e_refs, o_refs):
        a = accs[0]
        sg = jax.nn.sigmoid(a)
        o_refs[0][...] = jnp.where(pl.program_id(1) < n_silu_tiles, a * sg, sg).astype(act)
    gts = _linear("proj_gates", [h], [w_ret], [], [], gates_epi,
                  jax.ShapeDtypeStruct((m, 3 * D_MODEL), act), _tile_spec(tm, tn), tm, tn,
                  n=3 * D_MODEL, w_col0=(n_rot + n_rv) // tn)

    o_nsa, r, states = attend(q, kv_raw, gates, rqk, rv, gts)

    def merge_epi(accs, e_refs, o_refs):
        o_refs[0][...] = (e_refs[0][...].astype(f32) * accs[0]
                          + e_refs[1][...].astype(f32) * accs[1]).astype(act)
    nt = D_MODEL // tn
    merged = _linear("merge", [o_nsa, r], [lw["w_nsa_proj"], lw["w_ret_proj"]], [gts, gts],
                     [_tile_spec(tm, tn, nt), _tile_spec(tm, tn, 2 * nt)], merge_epi,
                     jax.ShapeDtypeStruct((m, D_MODEL), act), _tile_spec(tm, tn), tm, tn)

    def resid_epi(accs, e_refs, o_refs):
        o_refs[0][...] = e_refs[0][...] + e_refs[1][...] * accs[0]
    x = _linear("out_proj", [merged], [lw["w_out"]], [x, g1],
                [_tile_spec(tm, tn), _mod_spec(g1, tm, tn, rows_per_batch)], resid_epi,
                jax.ShapeDtypeStruct((m, D_MODEL), f32), _tile_spec(tm, tn), tm, tn)

    h2 = _norm("norm_mlp", x, lw["norm_mlp"], sc2, sh2, rows_per_batch, act)

    def up_epi(accs, e_refs, o_refs):
        o_refs[0][...] = jnp.square(jnp.maximum(accs[0], 0.0)).astype(act)
    u = _linear("mlp_up", [h2], [lw["w_up"]], [], [], up_epi,
                jax.ShapeDtypeStruct((m, D_FF), act), _tile_spec(tm, tn), tm, tn)
    x = _linear("mlp_down", [u], [lw["w_down"]], [x, g2],
                [_tile_spec(tm, tn), _mod_spec(g2, tm, tn, rows_per_batch)], resid_epi,
                jax.ShapeDtypeStruct((m, D_MODEL), f32), _tile_spec(tm, tn), tm, tn, tk=LINEAR_TK)
    return x, states


def kernel(x_prompt, x_sample, c_prompt, c_sample, cache_cmp_kv, cache_sel_kv, state_win_kv, state_ret,
           page_table, rel_bias, w_mod, b_mod, norm_mix, norm_mlp, w_in, cmp_pos, w_phi, w_nsa_proj,
           ret_gn_w, ret_gn_b, w_ret_proj, w_out, w_up, w_down, norm_final):
    batch, seq, _ = x_prompt.shape
    dbatch, dseq, _ = x_sample.shape
    depth = w_in.shape[0]
    n_pool = cache_cmp_kv.shape[1]
    n_pages = page_table.shape[1]
    i32 = jnp.int32

    n_c = batch + dbatch
    c_all = jnp.pad(jnp.concatenate([c_prompt, c_sample], axis=0), ((0, (-n_c) % SUBLANES), (0, 0)))
    mod_all = _mod_all(c_all, w_mod, b_mod)

    nq = seq // QB
    n_cmp_p = seq // CMP_BLOCK
    kj = jnp.arange(KB, dtype=i32)[:, None]
    qi = jnp.arange(QB, dtype=i32)[None, :]
    dist_tiles = (jnp.arange(N_BIAS_TILES, dtype=i32)[:, None, None] * KB + qi[None] - kj[None])
    bias_t = _bias_from_dist(dist_tiles.reshape(N_BIAS_TILES * KB, QB), rel_bias, LOG2E)
    bias_t = bias_t.reshape(N_KV, HPG, N_BIAS_TILES, KB, QB).transpose(2, 0, 3, 1, 4).reshape(
        N_BIAS_TILES, N_KV, KB, HPG * QB)
    t_all = jnp.arange(seq, dtype=i32)
    c_end_p = jnp.arange(n_cmp_p, dtype=i32) * CMP_BLOCK + (CMP_BLOCK - 1)
    bias_c = _bias_from_dist(t_all[None, :] - c_end_p[:, None], rel_bias, LOG2E)
    bias_c = bias_c.reshape(N_KV, HPG, n_cmp_p, nq, QB).transpose(3, 0, 2, 1, 4).reshape(
        nq, N_KV, n_cmp_p, HPG * QB)

    cos_p, sin_p = _rope_tables(jnp.arange(seq, dtype=i32))
    dec_p = _retention_decays(RET_CHUNK)

    cols = HPG * N_KV * dseq
    t_cols = PAST_LEN + jnp.tile(jnp.arange(dseq, dtype=i32), HPG * N_KV)
    head_cols = jnp.tile(jnp.repeat(jnp.arange(N_KV, dtype=i32), dseq), HPG) * HPG + jnp.repeat(
        jnp.arange(HPG, dtype=i32), N_KV * dseq)
    t_new = PAST_LEN + jnp.arange(dseq, dtype=i32)

    def col_bias(key_pos):
        per_head = _bias_from_dist(t_new[:, None] - key_pos[None, :], rel_bias)
        t_idx = jnp.tile(jnp.arange(dseq, dtype=i32), HPG * N_KV)
        return per_head[head_cols, t_idx, :]

    n_cmp_s = PAST_LEN // CMP_BLOCK
    bias_c_s = col_bias(jnp.arange(n_cmp_s, dtype=i32) * CMP_BLOCK + (CMP_BLOCK - 1)).T
    bias_s_s = col_bias(jnp.arange(PAST_LEN, dtype=i32))
    bias_new_s = col_bias(PAST_LEN + jnp.arange(PAGE_SIZE, dtype=i32))
    w_buf = state_win_kv.shape[2]
    n_win_pad = -(-(w_buf + dseq) // LANES) * LANES
    bias_w_s = col_bias(PAST_LEN - w_buf + jnp.arange(n_win_pad, dtype=i32))
    n_sel_s = -(-(PAST_LEN + dseq) // SEL_BLOCK)
    n_sel_pad = -(-(n_pages * (PAGE_SIZE // SEL_BLOCK) + PAGE_SIZE // SEL_BLOCK) // SUBLANES) * SUBLANES
    assert n_sel_pad >= n_sel_s
    cos_s, sin_s = _rope_tables(t_new)
    cos_s, sin_s = jnp.tile(cos_s, (dbatch, 1)), jnp.tile(sin_s, (dbatch, 1))
    dec_s = _retention_decays(dseq)
    t_pos_row = t_cols.reshape(1, cols)
    t_pos_col = t_cols.reshape(cols, 1)

    page_major = lambda c: jnp.transpose(c, (0, 1, 3, 4, 5, 2)).reshape(depth, n_pool, PAIR_W, PAGE_SIZE)
    cache_cmp_t = page_major(cache_cmp_kv)
    cache_sel_t = page_major(cache_sel_kv)

    m_p = batch * seq
    m_s = dbatch * dseq
    tm_p = 1024
    tm_s = m_s

    xp = x_prompt.reshape(m_p, D_MODEL)
    xs = x_sample.reshape(m_s, D_MODEL)
    st_p, st_s = [], []
    for l in range(depth):
        w_attn, w_ng, w_ret = _prep_w_in(w_in[l])
        wk, wv, pos_row = _prep_w_phi(w_phi[l], cmp_pos[l])
        lw = dict(norm_mix=norm_mix[l], norm_mlp=norm_mlp[l], w_attn=w_attn, w_ng=w_ng, w_ret=w_ret,
                  w_nsa_proj=w_nsa_proj[l].astype(bf16), w_ret_proj=w_ret_proj[l].astype(bf16),
                  w_out=w_out[l].astype(bf16), w_up=w_up[l].astype(bf16), w_down=w_down[l].astype(bf16))
        gn_w = ret_gn_w[l].reshape(1, -1)
        gn_b = ret_gn_b[l].reshape(1, -1)

        mod_p = [a.reshape(batch, 1, D_MODEL) for a in jnp.split(mod_all[l, :batch], 6, axis=-1)]

        def attend_prompt(q, kv, gates, rqk, rv, gts):
            kv_raw, k_rows, vt_rows, raw_t = kv
            kc, vct = _compress_prompt(kv_raw, pos_row, wk, wv, batch)
            o = _nsa_prompt(q, kc, vct, k_rows, vt_rows, gates, bias_c, bias_t, batch, seq)
            zero_state = jnp.zeros((batch, RET_HEADS, RET_DK, RET_DV), f32)
            r, s_new = _retention(rqk, rv, gts, gn_w, gn_b, zero_state, *dec_p, batch, seq, RET_CHUNK, bf16)
            return o, r, (raw_t[0], raw_t[1], raw_t[2][:, :, seq - min(WINDOW, seq):], s_new)

        rope_p = (cos_p, sin_p, pl.BlockSpec((tm_p, RET_DK // 2), lambda i, j: (i % (seq // tm_p), 0)))
        xp, st = _layer(xp, mod_p, lw, seq, tm_p, bf16, HEAD_DIM ** -0.5 * LOG2E, True, rope_p, attend_prompt)
        st_p.append(st)

        mod_rows = jnp.repeat(mod_all[l, batch:batch + dbatch], dseq, axis=0)
        mod_s = jnp.split(mod_rows, 6, axis=-1)

        def attend_sample(q, kv_raw, gates, rqk, rv, gts):
            cmp_new, sel_new, win_new = (kv_raw[j].reshape(dbatch, dseq, PAIR_W) for j in range(3))
            kc, vct = _compress_sample(cache_cmp_t, l, page_table, pos_row, wk, wv)
            q5 = q.reshape(dbatch, dseq, N_KV, HPG, HEAD_DIM)
            qp = jnp.einsum('btghd,gk->bhgtkd', q5, jnp.eye(N_KV, dtype=q.dtype)).reshape(
                dbatch, cols, KV_W).astype(bf16)
            oc_t, sel_mask = _nsa_sample_cmp(kc, vct, jnp.swapaxes(qp, 1, 2), bias_c_s, t_pos_row,
                                             n_sel_s, n_sel_pad)
            g4 = gates[:, :N_KV * NG_SLOT].reshape(dbatch, dseq, N_KV, NG_SLOT)[..., :3 * HPG]
            g4 = g4.reshape(dbatch, dseq, N_KV, HPG, 3)
            gates_c = jnp.transpose(g4, (0, 3, 2, 1, 4)).reshape(dbatch, cols, 3)
            gates_c = jnp.pad(gates_c, ((0, 0), (0, 0), (0, LANES - 3)))
            sel_pad = jnp.pad(sel_new, ((0, 0), (0, PAGE_SIZE - dseq), (0, 0)))
            win_all = jnp.concatenate([state_win_kv[l].reshape(dbatch, w_buf, PAIR_W), win_new], axis=1)
            win_pad = jnp.pad(win_all, ((0, 0), (0, n_win_pad - w_buf - dseq), (0, 0)))
            o_q = _nsa_sample_sel(cache_sel_t, l, page_table, sel_pad, win_pad, qp, sel_mask, oc_t,
                                  gates_c, bias_s_s, bias_new_s, bias_w_s, t_pos_col)
            o6 = o_q.reshape(dbatch, HPG, N_KV, dseq, N_KV, HEAD_DIM)
            o = jnp.einsum('bhgtgd->btghd', o6).reshape(m_s, N_HEADS * HEAD_DIM)
            r, s_new = _retention(rqk, rv, gts, gn_w, gn_b, state_ret[l], *dec_s, dbatch, dseq, dseq, f32)
            kv6 = lambda a: a.reshape(dbatch, -1, 2, N_KV, HEAD_DIM)
            return o, r, (kv6(cmp_new), kv6(sel_new), kv6(win_all[:, -w_buf:]), s_new)

        rope_s = (cos_s, sin_s, pl.BlockSpec((tm_s, RET_DK // 2), lambda i, j: (i, 0)))
        xs, st = _layer(xs, mod_s, lw, dseq, tm_s, f32, HEAD_DIM ** -0.5, False, rope_s, attend_sample)
        st_s.append(st)

    y_p = _norm("norm_final", xp, norm_final, None, None, seq, f32).reshape(batch, seq, D_MODEL)
    y_s = _norm("norm_final", xs, norm_final, None, None, dseq, f32).reshape(dbatch, dseq, D_MODEL)
    cmp_p, sel_p, win_p, ret_p = [jnp.stack([st[i] for st in st_p]) for i in range(4)]
    rows_last = lambda a: jnp.transpose(a.reshape(depth, batch, 2, N_KV, HEAD_DIM, a.shape[-1]), (0, 1, 5, 2, 3, 4))
    cmp_p, sel_p, win_p = rows_last(cmp_p), rows_last(sel_p), rows_last(win_p)
    cmp_s, sel_s, win_s, ret_s = [jnp.stack([st[i] for st in st_s]) for i in range(4)]
    return (y_p, y_s, cmp_p, sel_p, win_p, ret_p, cmp_s, sel_s, win_s, ret_s)
```
